```python
import math
import jax
import jax.numpy as jnp
from jax import lax
import numpy as np

D_MODEL = 2048
BATCH = 2
SEQ = 16384
DEPTH = 2

N_BRANCHES = 4
BRANCH_WIDTH = D_MODEL // 4
HEAD_DIM = 64
IN_WIDTH = 9 * BRANCH_WIDTH
DIFF_HEADS = BRANCH_WIDTH // (2 * HEAD_DIM)
POOL_WINDOWS = (2, 4, 8, 16)
POOL_GROUPS = len(POOL_WINDOWS)
POOL_GROUP_DIM = BRANCH_WIDTH // POOL_GROUPS
POOL_MAX_WINDOW = max(POOL_WINDOWS)
SGU_CHUNK = 128
SGU_GROUPS = 4
SGU_GROUP_DIM = BRANCH_WIDTH // SGU_GROUPS
MOBA_HEADS = BRANCH_WIDTH // HEAD_DIM
MOBA_BLOCK = 256
MOBA_TOPK = 3
MOBA_Q_CHUNK = 64
ATTN_Q_BLOCK = 128
REL_BUCKETS = 32
REL_MAX_DIST = 128
N_ATTN_HEADS = DIFF_HEADS + MOBA_HEADS
N_EXPERTS = 16
N_EXPERT_GROUPS = 4
EXPERTS_PER_GROUP = N_EXPERTS // N_EXPERT_GROUPS
TOP_K = 2
EXPERT_FF = 1408
MOE_BLOCK = 512
ALPHA = (2 * DEPTH) ** 0.25
BETA = (8 * DEPTH) ** -0.25
NORM_EPS = 1e-5

kernel_name = 'hybrid_gated_mixers_grouped_moe'


def layer_norm(x, g, b):
    xf = x.astype(jnp.float32)
    mu = jnp.mean(xf, -1, keepdims=True)
    var = jnp.mean(jnp.square(xf - mu), -1, keepdims=True)
    return ((xf - mu) * lax.rsqrt(var + NORM_EPS) * g + b).astype(x.dtype)


def rms_norm(x, g):
    xf = x.astype(jnp.float32)
    return (xf * lax.rsqrt(jnp.mean(jnp.square(xf), -1, keepdims=True) + NORM_EPS) * g).astype(x.dtype)


def t5_bucket(dist):
    max_exact = REL_BUCKETS // 2
    d = jnp.maximum(dist, 1).astype(jnp.float32)
    large = max_exact + (jnp.log(d / max_exact) / math.log(REL_MAX_DIST / max_exact)
                         * (REL_BUCKETS - max_exact)).astype(jnp.int32)
    large = jnp.minimum(large, REL_BUCKETS - 1)
    return jnp.where(dist < max_exact, dist, large)


def diff_attention(q, k, v, bias_dist, lam):
    _, b, h, s, dh = q.shape
    nqb = s // ATTN_Q_BLOCK
    q_blocks = q.reshape(2, b, h, nqb, ATTN_Q_BLOCK, dh).transpose(3, 0, 1, 2, 4, 5)
    starts = jnp.arange(nqb, dtype=jnp.int32) * ATTN_Q_BLOCK
    k_pos = jnp.arange(s, dtype=jnp.int32)
    scale = dh ** -0.5

    def one_block(args):
        qb, start = args
        q_pos = start + jnp.arange(ATTN_Q_BLOCK, dtype=jnp.int32)
        dist = q_pos[:, None] - k_pos[None, :]
        bias = bias_dist[:, jnp.clip(dist, 0, s - 1)]
        logits = jnp.einsum('mbhqd,mbhkd->mbhqk', qb, k).astype(jnp.float32) * scale + bias
        logits = jnp.where(dist >= 0, logits, -jnp.inf)
        p = jax.nn.softmax(logits, axis=-1)
        p = p[0] - lam * p[1]
        return jnp.einsum('bhqk,bhkv->bhqv', p.astype(v.dtype), v)

    out = lax.map(one_block, (q_blocks, starts))
    return out.transpose(1, 2, 0, 3, 4).reshape(b, h, s, v.shape[-1])


def moba_attention(q, k, v, bias_dist):
    b, h, s, dh = q.shape
    nb = -(-s // MOBA_BLOCK)
    pad = nb * MOBA_BLOCK - s
    k_pad = jnp.pad(k, ((0, 0), (0, 0), (0, pad), (0, 0)))
    v_pad = jnp.pad(v, ((0, 0), (0, 0), (0, pad), (0, 0)))
    k_blocks = k_pad.reshape(b, h, nb, MOBA_BLOCK, dh)
    v_blocks = v_pad.reshape(b, h, nb, MOBA_BLOCK, dh)
    k_mean = jnp.mean(k_blocks.astype(jnp.float32), axis=3)
    n_sel = min(MOBA_TOPK, nb)
    nqc = s // MOBA_Q_CHUNK
    q_chunks = q.reshape(b, h, nqc, MOBA_Q_CHUNK, dh).transpose(2, 0, 1, 3, 4)
    starts = jnp.arange(nqc, dtype=jnp.int32) * MOBA_Q_CHUNK
    b_idx = jnp.arange(b)[:, None, None, None]
    h_idx = jnp.arange(h)[:, None, None]
    blk_pos = jnp.arange(MOBA_BLOCK, dtype=jnp.int32)
    block_ids = jnp.arange(nb, dtype=jnp.int32)
    scale = dh ** -0.5

    def one_chunk(args):
        qc, start = args
        q_pos = start + jnp.arange(MOBA_Q_CHUNK, dtype=jnp.int32)
        own = start // MOBA_BLOCK
        gate = jnp.einsum('bhqd,bhnd->bhqn', qc.astype(jnp.float32), k_mean)
        gate = jnp.where(block_ids < own, gate, -jnp.inf)
        top_val, top_idx = lax.top_k(gate, n_sel)
        sel_ok = jnp.isfinite(top_val)
        k_sel = k_blocks[b_idx, h_idx, top_idx]
        v_sel = v_blocks[b_idx, h_idx, top_idx]
        sel_dist = q_pos[:, None, None] - (top_idx[..., None] * MOBA_BLOCK + blk_pos)
        sel_bias = bias_dist[h_idx[..., None], jnp.clip(sel_dist, 0, s - 1)]
        sel_logits = jnp.einsum('bhqd,bhqnjd->bhqnj', qc, k_sel).astype(jnp.float32) * scale + sel_bias
        sel_logits = jnp.where(sel_ok[..., None], sel_logits, -jnp.inf)
        sel_logits = sel_logits.reshape(b, h, MOBA_Q_CHUNK, n_sel * MOBA_BLOCK)
        k_own = lax.dynamic_slice_in_dim(k_pad, own * MOBA_BLOCK, MOBA_BLOCK, axis=2)
        v_own = lax.dynamic_slice_in_dim(v_pad, own * MOBA_BLOCK, MOBA_BLOCK, axis=2)
        own_dist = q_pos[:, None] - (own * MOBA_BLOCK + blk_pos)[None, :]
        own_bias = bias_dist[:, jnp.clip(own_dist, 0, s - 1)]
        own_logits = jnp.einsum('bhqd,bhjd->bhqj', qc, k_own).astype(jnp.float32) * scale + own_bias
        own_logits = jnp.where(own_dist >= 0, own_logits, -jnp.inf)
        p = jax.nn.softmax(jnp.concatenate([sel_logits, own_logits], -1), axis=-1)
        p_sel = p[..., :n_sel * MOBA_BLOCK].reshape(b, h, MOBA_Q_CHUNK, n_sel, MOBA_BLOCK).astype(v.dtype)
        p_own = p[..., n_sel * MOBA_BLOCK:].astype(v.dtype)
        return (jnp.einsum('bhqnj,bhqnjd->bhqd', p_sel, v_sel)
                + jnp.einsum('bhqj,bhjd->bhqd', p_own, v_own))

    out = lax.map(one_chunk, (q_chunks, starts))
    return out.transpose(1, 2, 0, 3, 4).reshape(b, h, s, dh)


def multiscale_pool(z):
    b, s, _ = z.shape
    zf = z.astype(jnp.float32)
    cs = jnp.pad(jnp.cumsum(zf, axis=1), ((0, 0), (POOL_MAX_WINDOW, 0), (0, 0)))
    t = jnp.arange(s)
    outs = []
    for g, w in enumerate(POOL_WINDOWS):
        sl = slice(g * POOL_GROUP_DIM, (g + 1) * POOL_GROUP_DIM)
        win_sum = cs[:, POOL_MAX_WINDOW:, sl] - cs[:, POOL_MAX_WINDOW - w:POOL_MAX_WINDOW - w + s, sl]
        cnt = jnp.minimum(t + 1, w).astype(jnp.float32)[None, :, None]
        outs.append(win_sum / cnt)
    return (jnp.concatenate(outs, -1) - zf).astype(z.dtype)


def spatial_gating(zu, zv, ln_g, ln_b, w_sp, b_sp):
    b, s, _ = zu.shape
    u = jax.nn.gelu(zu)
    v = layer_norm(jax.nn.gelu(zv), ln_g, ln_b)
    vb = v.reshape(b, s // SGU_CHUNK, SGU_CHUNK, SGU_GROUPS, SGU_GROUP_DIM)
    w_causal = w_sp * jnp.tril(jnp.ones((SGU_CHUNK, SGU_CHUNK), w_sp.dtype))
    vm = jnp.einsum('gts,bnsgc->bntgc', w_causal, vb) + b_sp.T[None, None, :, :, None]
    return u * vm.reshape(b, s, BRANCH_WIDTH)


def grouped_moe(h, w_router, b_router, w_g, w_u, w_d):
    b, s, d = h.shape
    t = b * s
    ht = h.reshape(t, d)
    aff = jax.nn.sigmoid(jnp.dot(ht, w_router).astype(jnp.float32))
    sel = aff + b_router.astype(jnp.float32)
    grp_score = jnp.sum(lax.top_k(sel.reshape(t, N_EXPERT_GROUPS, EXPERTS_PER_GROUP), TOP_K)[0], -1)
    grp = jnp.argmax(grp_score, axis=-1)
    in_grp = (jnp.arange(N_EXPERTS) // EXPERTS_PER_GROUP)[None, :] == grp[:, None]
    _, idx = lax.top_k(jnp.where(in_grp, sel, -jnp.inf), TOP_K)
    wts = jnp.take_along_axis(aff, idx, axis=-1)
    wts = wts / jnp.sum(wts, -1, keepdims=True)
    n = t * TOP_K
    e_flat = idx.reshape(n)
    tok_flat = jnp.arange(n, dtype=jnp.int32) // TOP_K
    order = jnp.argsort(e_flat)
    e_sorted = e_flat[order]
    counts = jnp.zeros((N_EXPERTS,), jnp.int32).at[e_flat].add(1)
    starts = jnp.cumsum(counts) - counts
    padded = (counts + MOE_BLOCK - 1) // MOE_BLOCK * MOE_BLOCK
    padded_end = jnp.cumsum(padded)
    dest = (padded_end[e_sorted] - padded[e_sorted]
            + jnp.arange(n, dtype=jnp.int32) - starts[e_sorted])
    n_blocks = -(-n // MOE_BLOCK) + N_EXPERTS
    buf_tok = jnp.full((n_blocks * MOE_BLOCK,), t, jnp.int32).at[dest].set(tok_flat[order])
    buf_w = jnp.zeros((n_blocks * MOE_BLOCK,), jnp.float32).at[dest].set(wts.reshape(n)[order])
    blk_expert = jnp.minimum(
        jnp.searchsorted(padded_end, jnp.arange(n_blocks, dtype=jnp.int32) * MOE_BLOCK, side='right'),
        N_EXPERTS - 1)
    h_pad = jnp.concatenate([ht, jnp.zeros((1, d), ht.dtype)], 0)

    def expert_block(args):
        tok, e = args
        xb = h_pad[tok]
        return jnp.dot(jax.nn.silu(jnp.dot(xb, w_g[e])) * jnp.dot(xb, w_u[e]), w_d[e])

    y = lax.map(expert_block, (buf_tok.reshape(n_blocks, MOE_BLOCK), blk_expert))
    y = y.reshape(n_blocks * MOE_BLOCK, d) * buf_w[:, None].astype(y.dtype)
    out = jnp.zeros((t + 1, d), y.dtype).at[buf_tok].add(y)[:t]
    return out.reshape(b, s, d)


def hybrid_mixer(h, layer, bias_dist, w_in, lam_q1, lam_k1, lam_q2, lam_k2, diff_norm_g,
                 w_pool, pool_scale, sgu_ln_g, sgu_ln_b, w_spatial, b_spatial,
                 w_branch, w_gate, b_gate, w_out):
    b, s, _ = h.shape
    z = jnp.einsum('bsd,de->bse', h, w_in)
    q_a, k_a, v_a, z_pool, z_u, z_v, q_d, k_d, v_d = jnp.split(z, 9, axis=-1)

    lam_init = 0.8 - 0.6 * math.exp(-0.3 * layer)
    lam = (jnp.exp(jnp.sum(lam_q1.astype(jnp.float32) * lam_k1.astype(jnp.float32)))
           - jnp.exp(jnp.sum(lam_q2.astype(jnp.float32) * lam_k2.astype(jnp.float32))) + lam_init)
    qa = q_a.reshape(b, s, DIFF_HEADS, 2, HEAD_DIM).transpose(3, 0, 2, 1, 4)
    ka = k_a.reshape(b, s, DIFF_HEADS, 2, HEAD_DIM).transpose(3, 0, 2, 1, 4)
    va = v_a.reshape(b, s, DIFF_HEADS, 2 * HEAD_DIM).transpose(0, 2, 1, 3)
    o_a = diff_attention(qa, ka, va, bias_dist[:DIFF_HEADS], lam)
    o_a = rms_norm(o_a, diff_norm_g) * (1.0 - lam_init)
    o_a = o_a.transpose(0, 2, 1, 3).reshape(b, s, BRANCH_WIDTH)

    pooled = multiscale_pool(z_pool).reshape(b, s, POOL_GROUPS, POOL_GROUP_DIM)
    o_b = jnp.einsum('bsgc,gce->bsge', pooled, w_pool).reshape(b, s, BRANCH_WIDTH) * pool_scale

    o_c = spatial_gating(z_u, z_v, sgu_ln_g, sgu_ln_b, w_spatial, b_spatial)

    qd, kd, vd = (t_.reshape(b, s, MOBA_HEADS, HEAD_DIM).transpose(0, 2, 1, 3) for t_ in (q_d, k_d, v_d))
    o_d = moba_attention(qd, kd, vd, bias_dist[DIFF_HEADS:])
    o_d = o_d.transpose(0, 2, 1, 3).reshape(b, s, BRANCH_WIDTH)

    merged = None
    for i, o in enumerate((o_a, o_b, o_c, o_d)):
        gate = jax.nn.sigmoid(jnp.einsum('bsd,de->bse', h, w_gate[i]) + b_gate[i])
        term = gate * jnp.einsum('bsc,cd->bsd', o, w_branch[i])
        merged = term if merged is None else merged + term
    return jnp.einsum('bsd,de->bse', merged, w_out)


def setup_inputs(seed: int = 0) -> dict:
    key = jax.random.key(seed)
    ks = iter(jax.random.split(key, 40))
    f32 = jnp.float32

    def nrm(shape, scale):
        return jax.random.normal(next(ks), shape, f32) * scale

    D, L, E, F, W = D_MODEL, DEPTH, N_EXPERTS, EXPERT_FF, BRANCH_WIDTH
    return {
        'x': nrm((BATCH, SEQ, D), 1.0),
        'c': nrm((BATCH, D), 1.0),
        'rel_bias': nrm((REL_BUCKETS, N_ATTN_HEADS), 0.2),
        'w_router': nrm((D, E), D ** -0.5),
        'b_router': nrm((E,), 0.01),
        'w_ada': nrm((L, D, 6 * D), 0.1 * D ** -0.5),
        'b_ada': nrm((L, 6 * D), 0.02),
        'w_in': nrm((L, D, IN_WIDTH), D ** -0.5),
        'lam_q1': nrm((L, HEAD_DIM), 0.1),
        'lam_k1': nrm((L, HEAD_DIM), 0.1),
        'lam_q2': nrm((L, HEAD_DIM), 0.1),
        'lam_k2': nrm((L, HEAD_DIM), 0.1),
        'diff_norm_g': 1.0 + nrm((L, 2 * HEAD_DIM), 0.02),
        'w_pool': nrm((L, POOL_GROUPS, POOL_GROUP_DIM, POOL_GROUP_DIM), POOL_GROUP_DIM ** -0.5),
        'pool_scale': 1.0 + nrm((L, W), 0.02),
        'sgu_ln_g': 1.0 + nrm((L, W), 0.02),
        'sgu_ln_b': nrm((L, W), 0.02),
        'w_spatial': nrm((L, SGU_GROUPS, SGU_CHUNK, SGU_CHUNK), SGU_CHUNK ** -0.5),
        'b_spatial': 1.0 + nrm((L, SGU_GROUPS, SGU_CHUNK), 0.02),
        'w_branch': nrm((L, N_BRANCHES, W, D), W ** -0.5),
        'w_gate': nrm((L, N_BRANCHES, D, D), D ** -0.5),
        'b_gate': nrm((L, N_BRANCHES, D), 0.02),
        'w_out': nrm((L, D, D), BETA * D ** -0.5),
        'ln1_g': 1.0 + nrm((L, D), 0.02),
        'ln1_b': nrm((L, D), 0.02),
        'w_e_gate': nrm((L, E, D, F), D ** -0.5),
        'w_e_up': nrm((L, E, D, F), D ** -0.5),
        'w_e_down': nrm((L, E, F, D), BETA * F ** -0.5),
        'ln2_g': 1.0 + nrm((L, D), 0.02),
        'ln2_b': nrm((L, D), 0.02),
    }


def reference(x, c, rel_bias, w_router, b_router, w_ada, b_ada, w_in, lam_q1, lam_k1, lam_q2,
              lam_k2, diff_norm_g, w_pool, pool_scale, sgu_ln_g, sgu_ln_b, w_spatial, b_spatial,
              w_branch, w_gate, b_gate, w_out, ln1_g, ln1_b, w_e_gate, w_e_up, w_e_down,
              ln2_g, ln2_b):
    s = x.shape[1]
    bias_dist = rel_bias[t5_bucket(jnp.arange(s, dtype=jnp.int32))].T
    c_act = jax.nn.silu(c)
    for layer in range(DEPTH):
        mod = jnp.dot(c_act, w_ada[layer]) + b_ada[layer]
        sh1, sc1, g1, sh2, sc2, g2 = (m[:, None, :] for m in jnp.split(mod, 6, axis=-1))
        h = x * (1.0 + sc1) + sh1
        y = hybrid_mixer(h, layer, bias_dist, w_in[layer], lam_q1[layer], lam_k1[layer],
                         lam_q2[layer], lam_k2[layer], diff_norm_g[layer], w_pool[layer],
                         pool_scale[layer], sgu_ln_g[layer], sgu_ln_b[layer], w_spatial[layer],
                         b_spatial[layer], w_branch[layer], w_gate[layer], b_gate[layer],
                         w_out[layer])
        x = layer_norm(ALPHA * x + (1.0 + g1) * y, ln1_g[layer], ln1_b[layer])
        h = x * (1.0 + sc2) + sh2
        y = grouped_moe(h, w_router, b_router, w_e_gate[layer], w_e_up[layer], w_e_down[layer])
        x = layer_norm(ALPHA * x + (1.0 + g2) * y, ln2_g[layer], ln2_b[layer])
    return x
```

```python
import functools
import math

import numpy as np
import jax
import jax.numpy as jnp
from jax import lax
from jax.experimental import pallas as pl
from jax.experimental.pallas import tpu as pltpu

F32 = jnp.float32
BF16 = jnp.bfloat16

D_MODEL = 2048
DEPTH = 2
BRANCH_WIDTH = 512
HEAD_DIM = 64
N_SECTIONS = 9
DIFF_HEADS = 4
POOL_WINDOWS = (2, 4, 8, 16)
POOL_MAX_WINDOW = 16
SGU_CHUNK = 128
SGU_GROUPS = 4
MOBA_HEADS = 8
MOBA_BLOCK = 256
MOBA_TOPK = 3
REL_BUCKETS = 32
REL_MAX_DIST = 128
N_EXPERTS = 16
N_EXPERT_GROUPS = 4
EXPERTS_PER_GROUP = 4
TOP_K = 2
EXPERT_FF = 1408
ALPHA = (2 * DEPTH) ** 0.25
NORM_EPS = 1e-5

LANES = 128
MASK_VALUE = -1e30
ATTN_TILE = 512
SLOT_BLOCK = 512
VMEM_LIMIT = 56 * 2 ** 20


def _cparams(semantics, vmem=VMEM_LIMIT):
    return pltpu.CompilerParams(dimension_semantics=semantics, vmem_limit_bytes=vmem)


def _sigmoid(x):
    return 1.0 / (1.0 + jnp.exp(-x))


def _layer_norm(r, g, b):
    mu = jnp.mean(r, -1, keepdims=True)
    var = jnp.mean(jnp.square(r - mu), -1, keepdims=True)
    return (r - mu) * lax.rsqrt(var + NORM_EPS) * g + b


def _ada_kernel(ct_ref, w_ref, b_ref, o_ref, *, batch):
    ct = ct_ref[...]
    ca = ct * _sigmoid(ct)
    w = w_ref[0]
    rows = [jnp.sum(w * ca[:, b:b + 1], axis=0, keepdims=True) for b in range(batch)]
    o_ref[0] = jnp.concatenate(rows, axis=0) + b_ref[0]


def _ada_mod(c, w_ada, b_ada):
    batch, d = c.shape
    depth, _, n = w_ada.shape
    tn = 512
    return pl.pallas_call(
        functools.partial(_ada_kernel, batch=batch),
        grid=(depth, n // tn),
        in_specs=[
            pl.BlockSpec((d, batch), lambda l, j: (0, 0)),
            pl.BlockSpec((1, d, tn), lambda l, j: (l, 0, j)),
            pl.BlockSpec((1, 1, tn), lambda l, j: (l, 0, j)),
        ],
        out_specs=pl.BlockSpec((1, batch, tn), lambda l, j: (l, 0, j)),
        out_shape=jax.ShapeDtypeStruct((depth, batch, n), F32),
        compiler_params=_cparams(("parallel", "parallel")),
        name="ada_mod",
    )(c.T, w_ada, b_ada.reshape(depth, 1, n))


def _inproj_kernel(x_ref, sc_ref, sh_ref, w_ref, h_ref, a_ref, p_ref, d_ref):
    w = BRANCH_WIDTH
    h = (x_ref[0] * (1.0 + sc_ref[0]) + sh_ref[0]).astype(BF16)
    h_ref[0] = h
    for s in range(N_SECTIONS):
        z = jnp.dot(h, w_ref[:, s * w:(s + 1) * w], preferred_element_type=F32)
        if s in (0, 6):
            z = z * (HEAD_DIM ** -0.5)
        if s < 3:
            a_ref[0, :, s * w:(s + 1) * w] = z.astype(BF16)
        elif s < 6:
            p_ref[0, :, (s - 3) * w:(s - 2) * w] = z
        else:
            d_ref[0, :, (s - 6) * w:(s - 5) * w] = z.astype(BF16)


def _inproj(x, sc, sh, w_in):
    b, s, d = x.shape
    tm = min(512, s)
    n3 = 3 * BRANCH_WIDTH
    row = lambda bi, i: (bi, i, 0)
    vec = lambda bi, i: (bi, 0, 0)
    return pl.pallas_call(
        _inproj_kernel,
        grid=(b, s // tm),
        in_specs=[
            pl.BlockSpec((1, tm, d), row),
            pl.BlockSpec((1, 1, d), vec),
            pl.BlockSpec((1, 1, d), vec),
            pl.BlockSpec(w_in.shape, lambda bi, i: (0, 0)),
        ],
        out_specs=[
            pl.BlockSpec((1, tm, d), row),
            pl.BlockSpec((1, tm, n3), row),
            pl.BlockSpec((1, tm, n3), row),
            pl.BlockSpec((1, tm, n3), row),
        ],
        out_shape=[
            jax.ShapeDtypeStruct((b, s, d), BF16),
            jax.ShapeDtypeStruct((b, s, n3), BF16),
            jax.ShapeDtypeStruct((b, s, n3), F32),
            jax.ShapeDtypeStruct((b, s, n3), BF16),
        ],
        compiler_params=_cparams(("parallel", "parallel")),
        name="inproj",
    )(x, sc, sh, w_in)


def _branches_kernel(z_ref, halo_ref, wp_ref, ps_ref, lg_ref, lb_ref, wsp_ref, bsp_ref,
                     ob_ref, oc_ref, *, tm):
    i = pl.program_id(1)
    w = BRANCH_WIDTH
    gd = w // len(POOL_WINDOWS)
    hw = POOL_MAX_WINDOW

    zp = z_ref[0, :, 0:w]
    halo = jnp.where(i > 0, halo_ref[0, :, 0:w], 0.0)
    ext = jnp.concatenate([halo, zp], axis=0)
    pos = (i * tm + lax.broadcasted_iota(jnp.int32, (tm, 1), 0) + 1).astype(F32)
    outs = []
    for g, win in enumerate(POOL_WINDOWS):
        e = ext[:, g * gd:(g + 1) * gd]
        shift = 1
        while shift < win:
            e = e + pltpu.roll(e, shift, axis=0)
            shift *= 2
        cnt = jnp.minimum(pos, float(win))
        pooled = e[hw:, :] / cnt - zp[:, g * gd:(g + 1) * gd]
        outs.append(jnp.dot(pooled.astype(BF16), wp_ref[g].astype(BF16), preferred_element_type=F32))
    ob_ref[0] = (jnp.concatenate(outs, axis=1) * ps_ref[...]).astype(BF16)

    u = jax.nn.gelu(z_ref[0, :, w:2 * w], approximate=True)
    v = _layer_norm(jax.nn.gelu(z_ref[0, :, 2 * w:3 * w], approximate=True), lg_ref[...], lb_ref[...])
    r = lax.broadcasted_iota(jnp.int32, (SGU_CHUNK, SGU_CHUNK), 0)
    c = lax.broadcasted_iota(jnp.int32, (SGU_CHUNK, SGU_CHUNK), 1)
    sg = w // SGU_GROUPS
    for g in range(SGU_GROUPS):
        wc = jnp.where(r >= c, wsp_ref[g], 0.0).astype(BF16)
        for ch in range(tm // SGU_CHUNK):
            rows = slice(ch * SGU_CHUNK, (ch + 1) * SGU_CHUNK)
            cols = slice(g * sg, (g + 1) * sg)
            vm = jnp.dot(wc, v[rows, cols].astype(BF16), preferred_element_type=F32) + bsp_ref[g]
            oc_ref[0, rows, cols] = (u[rows, cols] * vm).astype(BF16)


def _branches(zpuv, w_pool, pool_scale, ln_g, ln_b, w_sp, b_sp):
    b, s, n3 = zpuv.shape
    tm = min(512, s)
    w = BRANCH_WIDTH
    hw = POOL_MAX_WINDOW
    bsp = jnp.broadcast_to(b_sp[:, :, None], (SGU_GROUPS, SGU_CHUNK, w // SGU_GROUPS))
    full = lambda a: pl.BlockSpec(a.shape, lambda bi, i: (0,) * a.ndim)
    ps, lg, lb = pool_scale.reshape(1, w), ln_g.reshape(1, w), ln_b.reshape(1, w)
    return pl.pallas_call(
        functools.partial(_branches_kernel, tm=tm),
        grid=(b, s // tm),
        in_specs=[
            pl.BlockSpec((1, tm, n3), lambda bi, i: (bi, i, 0)),
            pl.BlockSpec((1, hw, n3), lambda bi, i: (bi, jnp.maximum(i * (tm // hw) - 1, 0), 0)),
            full(w_pool), full(ps), full(lg), full(lb), full(w_sp), full(bsp),
        ],
        out_specs=[pl.BlockSpec((1, tm, w), lambda bi, i: (bi, i, 0))] * 2,
        out_shape=[jax.ShapeDtypeStruct((b, s, w), BF16)] * 2,
        compiler_params=_cparams(("parallel", "parallel")),
        name="pool_sgu",
    )(zpuv, zpuv, w_pool, ps, lg, lb, w_sp, bsp)


def _softmax_step(s, vb, m_sc, l_sc, acc_sc):
    m_old = m_sc[...]
    m_new = jnp.maximum(m_old, jnp.max(s, axis=-1, keepdims=True))
    alpha = jnp.exp(m_old - m_new)
    p = jnp.exp(s - m_new)
    l_sc[...] = alpha * l_sc[...] + jnp.sum(p, axis=-1, keepdims=True)
    acc_sc[...] = alpha * acc_sc[...] + jnp.dot(p.astype(BF16), vb, preferred_element_type=F32)
    m_sc[...] = m_new


def _init_softmax(m_sc, l_sc, acc_sc):
    m_sc[...] = jnp.full(m_sc.shape, MASK_VALUE, F32)
    l_sc[...] = jnp.zeros(l_sc.shape, F32)
    acc_sc[...] = jnp.zeros(acc_sc.shape, F32)


_NT = (((1,), (1,)), ((), ()))


def _diff_attn_kernel(lam_ref, g_ref, q_ref, k_ref, v_ref, bd_ref, bp_ref, o_ref,
                      m_sc, l_sc, acc_sc, *, t, lam_init):
    i = pl.program_id(2)
    q = q_ref[0]
    lane = lax.broadcasted_iota(jnp.int32, (1, LANES), 1)
    zero = jnp.zeros_like(q)
    qq = jnp.concatenate([jnp.where(lane < HEAD_DIM, q, zero),
                          jnp.where(lane >= HEAD_DIM, q, zero)], axis=0)
    _init_softmax(m_sc, l_sc, acc_sc)

    def step(j, bias):
        start = pl.multiple_of(j * t, t)
        kb = k_ref[0, pl.ds(start, t), :]
        vb = v_ref[0, pl.ds(start, t), :]
        s = lax.dot_general(qq, kb, _NT, preferred_element_type=F32)
        if bias is not None:
            s = (s.reshape(2, t, t) + bias[None]).reshape(2 * t, t)
        _softmax_step(s, vb, m_sc, l_sc, acc_sc)

    def far_step(j, carry):
        step(j, None)
        return carry

    lax.fori_loop(0, jnp.maximum(i - 1, 0), far_step, 0)

    @pl.when(i >= 1)
    def _():
        step(i - 1, bp_ref[0])

    step(i, bd_ref[0])

    o = acc_sc[...] / l_sc[...]
    lam = (jnp.exp(jnp.sum(lam_ref[0:1, :] * lam_ref[1:2, :], axis=-1, keepdims=True))
           - jnp.exp(jnp.sum(lam_ref[2:3, :] * lam_ref[3:4, :], axis=-1, keepdims=True)) + lam_init)
    od = o[:t] - lam * o[t:]
    ms = jnp.mean(jnp.square(od), -1, keepdims=True)
    o_ref[0] = ((od * lax.rsqrt(ms + NORM_EPS) * g_ref[...]) * (1.0 - lam_init)).astype(BF16)


def _diff_attention(qkv, lams, norm_g, bias_diag, bias_prev, lam_init):
    b, s, _ = qkv.shape
    t = bias_diag.shape[-1]
    nh = DIFF_HEADS
    return pl.pallas_call(
        functools.partial(_diff_attn_kernel, t=t, lam_init=lam_init),
        grid=(b, nh, s // t),
        in_specs=[
            pl.BlockSpec(lams.shape, lambda bi, h, i: (0, 0)),
            pl.BlockSpec(norm_g.shape, lambda bi, h, i: (0, 0)),
            pl.BlockSpec((1, t, LANES), lambda bi, h, i: (bi, i, h)),
            pl.BlockSpec((1, s, LANES), lambda bi, h, i: (bi, 0, nh + h)),
            pl.BlockSpec((1, s, LANES), lambda bi, h, i: (bi, 0, 2 * nh + h)),
            pl.BlockSpec((1, t, t), lambda bi, h, i: (h, 0, 0)),
            pl.BlockSpec((1, t, t), lambda bi, h, i: (h, 0, 0)),
        ],
        out_specs=pl.BlockSpec((1, t, LANES), lambda bi, h, i: (bi, i, h)),
        out_shape=jax.ShapeDtypeStruct((b, s, BRANCH_WIDTH), BF16),
        scratch_shapes=[pltpu.VMEM((2 * t, 1), F32), pltpu.VMEM((2 * t, 1), F32),
                        pltpu.VMEM((2 * t, LANES), F32)],
        compiler_params=_cparams(("parallel", "parallel", "arbitrary")),
        name="diff_attention",
    )(lams, norm_g, qkv, qkv, qkv, bias_diag, bias_prev)


def _kmean_kernel(k_ref, o_ref):
    o_ref[0, 0] = jnp.mean(k_ref[0].astype(F32), axis=0, keepdims=True)


def _block_means(qkv):
    b, s, _ = qkv.shape
    nb = s // MOBA_BLOCK
    w = BRANCH_WIDTH
    return pl.pallas_call(
        _kmean_kernel,
        grid=(b, nb),
        in_specs=[pl.BlockSpec((1, MOBA_BLOCK, w), lambda bi, j: (bi, j, 1))],
        out_specs=pl.BlockSpec((1, 1, 1, w), lambda bi, j: (bi, j, 0, 0)),
        out_shape=jax.ShapeDtypeStruct((b, nb, 1, w), F32),
        compiler_params=_cparams(("parallel", "parallel")),
        name="moba_kmean",
    )(qkv)


def _moba_kernel(q_ref, km_ref, k_ref, v_ref, bd_ref, bp_ref, o_ref, m_sc, l_sc, acc_sc, *, t):
    i = pl.program_id(2)
    q = q_ref[0]
    lane = lax.broadcasted_iota(jnp.int32, (1, LANES), 1)
    blk = lax.broadcasted_iota(jnp.int32, (t, LANES), 1).astype(F32)
    zero = jnp.zeros_like(q)
    km = km_ref[0].astype(BF16)
    ninf = jnp.float32(-jnp.inf)
    aug = []
    for hh in range(2):
        qh = jnp.where((lane >= hh * HEAD_DIM) & (lane < (hh + 1) * HEAD_DIM), q, zero)
        gate = lax.dot_general(qh, km, _NT, preferred_element_type=F32)
        g = jnp.where(blk < i.astype(F32), gate, ninf)
        sel = jnp.zeros((t, LANES), jnp.bool_)
        for _ in range(MOBA_TOPK):
            mx = jnp.max(g, axis=-1, keepdims=True)
            first = jnp.min(jnp.where(g == mx, blk, float(LANES)), axis=-1, keepdims=True)
            hit = blk == first
            sel = sel | (hit & (mx > ninf))
            g = jnp.where(hit, ninf, g)
        aug.append(jnp.concatenate([qh, jnp.where(sel, 0.0, MASK_VALUE).astype(BF16)], axis=1))
    qa = jnp.concatenate(aug, axis=0)
    _init_softmax(m_sc, l_sc, acc_sc)

    def step(j, bias, own):
        start = pl.multiple_of(j * t, t)
        kb = k_ref[0, pl.ds(start, t), :]
        vb = v_ref[0, pl.ds(start, t), :]
        if own:
            s = lax.dot_general(qa[:, :LANES], kb, _NT, preferred_element_type=F32)
        else:
            onehot = jnp.broadcast_to(jnp.where(lane == j, 1.0, 0.0).astype(BF16), (t, LANES))
            s = lax.dot_general(qa, jnp.concatenate([kb, onehot], axis=1), _NT,
                                preferred_element_type=F32)
        if bias is not None:
            s = s + bias
        _softmax_step(s, vb, m_sc, l_sc, acc_sc)

    def far_step(j, carry):
        step(j, None, False)
        return carry

    lax.fori_loop(0, jnp.maximum(i - 1, 0), far_step, 0)

    @pl.when(i >= 1)
    def _():
        step(i - 1, bp_ref[0], False)

    step(i, bd_ref[0], True)

    o = acc_sc[...] / l_sc[...]
    o_ref[0] = jnp.where(lane < HEAD_DIM, o[:t], o[t:]).astype(BF16)


def _moba_attention(qkv, kmean, bias_diag, bias_prev):
    b, s, _ = qkv.shape
    t = MOBA_BLOCK
    npair = MOBA_HEADS // 2
    return pl.pallas_call(
        functools.partial(_moba_kernel, t=t),
        grid=(b, npair, s // t),
        in_specs=[
            pl.BlockSpec((1, t, LANES), lambda bi, p, i: (bi, i, p)),
            pl.BlockSpec((1, LANES, LANES), lambda bi, p, i: (bi, 0, p)),
            pl.BlockSpec((1, s, LANES), lambda bi, p, i: (bi, 0, npair + p)),
            pl.BlockSpec((1, s, LANES), lambda bi, p, i: (bi, 0, 2 * npair + p)),
            pl.BlockSpec((1, 2 * t, t), lambda bi, p, i: (p, 0, 0)),
            pl.BlockSpec((1, 2 * t, t), lambda bi, p, i: (p, 0, 0)),
        ],
        out_specs=pl.BlockSpec((1, t, LANES), lambda bi, p, i: (bi, i, p)),
        out_shape=jax.ShapeDtypeStruct((b, s, BRANCH_WIDTH), BF16),
        scratch_shapes=[pltpu.VMEM((2 * t, 1), F32), pltpu.VMEM((2 * t, 1), F32),
                        pltpu.VMEM((2 * t, LANES), F32)],
        compiler_params=_cparams(("parallel", "parallel", "arbitrary")),
        name="moba_attention",
    )(qkv, kmean, qkv, qkv, bias_diag, bias_prev)


def _merge_kernel(h_ref, oa_ref, ob_ref, oc_ref, od_ref, wg_ref, bg_ref, wb_ref, out_ref):
    h = h_ref[...]
    acc = None
    for br, o_ref in enumerate((oa_ref, ob_ref, oc_ref, od_ref)):
        gate = _sigmoid(jnp.dot(h, wg_ref[br], preferred_element_type=F32) + bg_ref[br])
        term = gate * jnp.dot(o_ref[...], wb_ref[br], preferred_element_type=F32)
        acc = term if acc is None else acc + term
    out_ref[...] = acc.astype(BF16)


def _merge(h, branches, w_gate, b_gate, w_branch):
    t, d = h.shape
    w = BRANCH_WIDTH
    tm = min(512, t)
    tn = 512
    nbr = len(branches)
    row = lambda j, i: (i, 0)
    return pl.pallas_call(
        _merge_kernel,
        grid=(d // tn, t // tm),
        in_specs=[pl.BlockSpec((tm, d), row)] + [pl.BlockSpec((tm, w), row)] * nbr + [
            pl.BlockSpec((nbr, d, tn), lambda j, i: (0, 0, j)),
            pl.BlockSpec((nbr, 1, tn), lambda j, i: (0, 0, j)),
            pl.BlockSpec((nbr, w, tn), lambda j, i: (0, 0, j)),
        ],
        out_specs=pl.BlockSpec((tm, tn), lambda j, i: (i, j)),
        out_shape=jax.ShapeDtypeStruct((t, d), BF16),
        compiler_params=_cparams(("parallel", "parallel")),
        name="gated_merge",
    )(h, *branches, w_gate, b_gate, w_branch)


def _route(aff, sel):
    ng, pg = N_EXPERT_GROUPS, EXPERTS_PER_GROUP
    srow = [sel[e:e + 1] for e in range(N_EXPERTS)]
    arow = [aff[e:e + 1] for e in range(N_EXPERTS)]
    best_score, best_grp = None, None
    for g in range(ng):
        rows = srow[g * pg:(g + 1) * pg]
        score = None
        for a in range(pg):
            for b in range(a + 1, pg):
                pair = rows[a] + rows[b]
                score = pair if score is None else jnp.maximum(score, pair)
        if best_score is None:
            best_score, best_grp = score, jnp.zeros_like(score, dtype=jnp.int32)
        else:
            better = score > best_score
            best_score = jnp.where(better, score, best_score)
            best_grp = jnp.where(better, g, best_grp)

    def pick(rows_by_group):
        out = rows_by_group[0]
        for g in range(1, ng):
            out = jnp.where(best_grp == g, rows_by_group[g], out)
        return out

    cand_s = [pick([srow[g * pg + k] for g in range(ng)]) for k in range(pg)]
    cand_a = [pick([arow[g * pg + k] for g in range(ng)]) for k in range(pg)]

    def argmax_first(vals, exclude):
        bv, bi, ba = None, None, None
        for k in range(pg):
            v = vals[k] if exclude is None else jnp.where(exclude == k, -jnp.inf, vals[k])
            if bv is None:
                bv, bi, ba = v, jnp.zeros_like(best_grp), cand_a[k]
            else:
                better = v > bv
                bv = jnp.where(better, v, bv)
                bi = jnp.where(better, k, bi)
                ba = jnp.where(better, cand_a[k], ba)
        return bi, ba

    i0, a0 = argmax_first(cand_s, None)
    i1, a1 = argmax_first(cand_s, i0)
    tot = a0 + a1
    return best_grp * pg + i0, best_grp * pg + i1, a0 / tot, a1 / tot


def _outproj_kernel(m_ref, x_ref, wo_ref, g1_ref, lg_ref, lb_ref, sc2_ref, sh2_ref, wr_ref, br_ref,
                    x1_ref, h2_ref, ri_ref, rw_ref, *, tm):
    y = jnp.dot(m_ref[0], wo_ref[...], preferred_element_type=F32)
    x1 = _layer_norm(ALPHA * x_ref[0] + (1.0 + g1_ref[0]) * y, lg_ref[...], lb_ref[...])
    x1_ref[0] = x1
    h2 = x1 * (1.0 + sc2_ref[0]) + sh2_ref[0]
    h2_ref[0] = h2
    logits = lax.dot_general(wr_ref[...], h2.astype(BF16), _NT, preferred_element_type=F32)
    aff = _sigmoid(logits)
    e0, e1, w0, w1 = _route(aff, aff + br_ref[...])
    ri_ref[0] = jnp.concatenate([e0, e1, jnp.zeros((6, tm), jnp.int32)], axis=0)
    rw_ref[0] = jnp.concatenate([w0, w1, jnp.zeros((6, tm), F32)], axis=0)


def _outproj(merged, x, w_out, g1, ln_g, ln_b, sc2, sh2, w_router_t, b_router):
    b, s, d = x.shape
    tm = min(512, s)
    row = lambda bi, i: (bi, i, 0)
    vec = lambda bi, i: (bi, 0, 0)
    full = lambda a: pl.BlockSpec(a.shape, lambda bi, i: (0,) * a.ndim)
    lg, lb = ln_g.reshape(1, d), ln_b.reshape(1, d)
    br = b_router.reshape(N_EXPERTS, 1)
    return pl.pallas_call(
        functools.partial(_outproj_kernel, tm=tm),
        grid=(b, s // tm),
        in_specs=[
            pl.BlockSpec((1, tm, d), row), pl.BlockSpec((1, tm, d), row), full(w_out),
            pl.BlockSpec((1, 1, d), vec), full(lg), full(lb),
            pl.BlockSpec((1, 1, d), vec), pl.BlockSpec((1, 1, d), vec),
            full(w_router_t), full(br),
        ],
        out_specs=[
            pl.BlockSpec((1, tm, d), row), pl.BlockSpec((1, tm, d), row),
            pl.BlockSpec((1, 8, tm), lambda bi, i: (bi, 0, i)),
            pl.BlockSpec((1, 8, tm), lambda bi, i: (bi, 0, i)),
        ],
        out_shape=[
            jax.ShapeDtypeStruct((b, s, d), F32), jax.ShapeDtypeStruct((b, s, d), F32),
            jax.ShapeDtypeStruct((b, 8, s), jnp.int32), jax.ShapeDtypeStruct((b, 8, s), F32),
        ],
        compiler_params=_cparams(("parallel", "parallel")),
        name="outproj_ln_router",
    )(merged.reshape(b, s, d), x, w_out, g1, lg, lb, sc2, sh2, w_router_t, br)


def _row_copy(src, src_row, dst, dst_row, sem):
    return pltpu.make_async_copy(src.at[pl.ds(src_row, 1)], dst.at[pl.ds(dst_row, 1)], sem)


def _dispatch_kernel(dest_ref, h2_ref, xs_in_ref, xs_ref, sem, *, rows):
    del xs_in_ref
    base = pl.program_id(0) * rows

    def issue(r, carry):
        for k in range(TOP_K):
            _row_copy(h2_ref, base + r, xs_ref, dest_ref[0, 0, TOP_K * r + k], sem).start()
        return carry

    def drain(r, carry):
        for k in range(TOP_K):
            _row_copy(h2_ref, 0, xs_ref, 0, sem).wait()
        return carry

    lax.fori_loop(0, rows, issue, 0)
    lax.fori_loop(0, rows, drain, 0)


def _dispatch(dest, h2, n_slots):
    t, d = h2.shape
    rows = min(512, t)
    return pl.pallas_call(
        functools.partial(_dispatch_kernel, rows=rows),
        grid=(t // rows,),
        in_specs=[
            pl.BlockSpec((1, 1, TOP_K * rows), lambda i: (i, 0, 0), memory_space=pltpu.SMEM),
            pl.BlockSpec(memory_space=pl.ANY),
            pl.BlockSpec(memory_space=pl.ANY),
        ],
        out_specs=pl.BlockSpec(memory_space=pl.ANY),
        out_shape=jax.ShapeDtypeStruct((n_slots, d), F32),
        scratch_shapes=[pltpu.SemaphoreType.DMA(())],
        input_output_aliases={2: 0},
        compiler_params=_cparams(("arbitrary",)),
        name="moe_dispatch",
    )(dest.reshape(t // rows, 1, TOP_K * rows), h2, jnp.zeros((n_slots, d), F32))


def _expert_kernel(be_ref, nu_ref, xs_ref, wg_ref, wu_ref, wd_ref, ys_ref):
    del be_ref
    used = pl.program_id(0) < nu_ref[0]

    @pl.when(used)
    def _():
        xb = xs_ref[...].astype(BF16)
        g = jnp.dot(xb, wg_ref[0], preferred_element_type=F32)
        u = jnp.dot(xb, wu_ref[0], preferred_element_type=F32)
        act = ((g * _sigmoid(g)) * u).astype(BF16)
        ys_ref[...] = jnp.dot(act, wd_ref[0], preferred_element_type=F32)

    @pl.when(jnp.logical_not(used))
    def _():
        ys_ref[...] = jnp.zeros(ys_ref.shape, F32)


def _expert_ffn(blk_expert, n_used, xs, w_g, w_u, w_d):
    n_slots, d = xs.shape
    f = w_g.shape[-1]
    blk = SLOT_BLOCK
    last = lambda b, nu: jnp.minimum(b, nu[0] - 1)
    wspec = lambda shape: pl.BlockSpec(shape, lambda b, be, nu: (be[last(b, nu)], 0, 0),
                                       pipeline_mode=pl.Buffered(1))
    return pl.pallas_call(
        _expert_kernel,
        grid_spec=pltpu.PrefetchScalarGridSpec(
            num_scalar_prefetch=2,
            grid=(n_slots // blk,),
            in_specs=[
                pl.BlockSpec((blk, d), lambda b, be, nu: (last(b, nu), 0)),
                wspec((1, d, f)), wspec((1, d, f)), wspec((1, f, d)),
            ],
            out_specs=pl.BlockSpec((blk, d), lambda b, be, nu: (b, 0)),
        ),
        out_shape=jax.ShapeDtypeStruct((n_slots, d), F32),
        compiler_params=_cparams(("arbitrary",)),
        name="moe_experts",
    )(blk_expert, n_used, xs, w_g, w_u, w_d)


def _combine_kernel(dest_ref, w_ref, x1_ref, g2_ref, lg_ref, lb_ref, ys_ref, o_ref, buf, sem, *, rows):
    def issue(r, carry):
        for k in range(TOP_K):
            _row_copy(ys_ref, dest_ref[0, 0, TOP_K * r + k], buf.at[k], r, sem).start()
        return carry

    def drain(r, carry):
        for k in range(TOP_K):
            _row_copy(ys_ref, 0, buf.at[k], 0, sem).wait()
        return carry

    lax.fori_loop(0, rows, issue, 0)
    lax.fori_loop(0, rows, drain, 0)
    y = buf[0] * w_ref[:, 0:1] + buf[1] * w_ref[:, 1:2]
    o_ref[0] = _layer_norm(ALPHA * x1_ref[0] + (1.0 + g2_ref[0]) * y, lg_ref[...], lb_ref[...])


def _combine(dest, wts, x1, g2, ln_g, ln_b, ys):
    b, s, d = x1.shape
    rows = min(256, s)
    nt = s // rows
    lg, lb = ln_g.reshape(1, d), ln_b.reshape(1, d)
    full = lambda a: pl.BlockSpec(a.shape, lambda bi, i: (0,) * a.ndim)
    return pl.pallas_call(
        functools.partial(_combine_kernel, rows=rows),
        grid=(b, nt),
        in_specs=[
            pl.BlockSpec((1, 1, TOP_K * rows), lambda bi, i: (bi * nt + i, 0, 0), memory_space=pltpu.SMEM),
            pl.BlockSpec((rows, TOP_K), lambda bi, i: (bi * nt + i, 0)),
            pl.BlockSpec((1, rows, d), lambda bi, i: (bi, i, 0)),
            pl.BlockSpec((1, 1, d), lambda bi, i: (bi, 0, 0)),
            full(lg), full(lb),
            pl.BlockSpec(memory_space=pl.ANY),
        ],
        out_specs=pl.BlockSpec((1, rows, d), lambda bi, i: (bi, i, 0)),
        out_shape=jax.ShapeDtypeStruct((b, s, d), F32),
        scratch_shapes=[pltpu.VMEM((TOP_K, rows, d), F32), pltpu.SemaphoreType.DMA(())],
        compiler_params=_cparams(("arbitrary", "arbitrary")),
        name="moe_combine_ln",
    )(dest.reshape(b * nt, 1, TOP_K * rows), wts, x1, g2, lg, lb, ys)


def _slot_plan(e_flat, n_slots):
    blk = SLOT_BLOCK
    onehot = (e_flat[:, None] == jnp.arange(N_EXPERTS, dtype=jnp.int32)[None, :]).astype(jnp.int32)
    csum = jnp.cumsum(onehot, axis=0)
    rank = jnp.sum(csum * onehot, axis=1) - 1
    counts = csum[-1]
    padded = (counts + blk - 1) // blk * blk
    pend = jnp.cumsum(padded)
    dest = (pend - padded)[e_flat] + rank
    nblk = n_slots // blk
    blk_expert = jnp.minimum(
        jnp.searchsorted(pend, jnp.arange(nblk, dtype=jnp.int32) * blk, side='right'), N_EXPERTS - 1)
    return dest.astype(jnp.int32), blk_expert.astype(jnp.int32), (pend[-1:] // blk).astype(jnp.int32)


def _t5_bucket(dist):
    max_exact = REL_BUCKETS // 2
    d = jnp.maximum(dist, 1).astype(F32)
    large = max_exact + (jnp.log(d / max_exact) / math.log(REL_MAX_DIST / max_exact)
                         * (REL_BUCKETS - max_exact)).astype(jnp.int32)
    large = jnp.minimum(large, REL_BUCKETS - 1)
    return jnp.where(dist < max_exact, dist, large)


def _bias_tiles(rel, t):
    s = rel.shape[1]
    r = jnp.arange(t, dtype=jnp.int32)[:, None]
    c = jnp.arange(t, dtype=jnp.int32)[None, :]
    dd = r - c
    diag = jnp.where(dd >= 0, rel[:, jnp.clip(dd, 0, s - 1)], MASK_VALUE)
    prev = rel[:, jnp.clip(dd + t, 0, s - 1)]
    return diag, prev


def kernel(x, c, rel_bias, w_router, b_router, w_ada, b_ada, w_in, lam_q1, lam_k1, lam_q2, lam_k2,
           diff_norm_g, w_pool, pool_scale, sgu_ln_g, sgu_ln_b, w_spatial, b_spatial, w_branch,
           w_gate, b_gate, w_out, ln1_g, ln1_b, w_e_gate, w_e_up, w_e_down, ln2_g, ln2_b):
    b, s, d = x.shape
    t = b * s
    assert d == D_MODEL and s % MOBA_BLOCK == 0 and s // MOBA_BLOCK <= LANES
    ta = min(ATTN_TILE, s)
    assert int(16 * 8 ** (15 / 16)) + 1 < MOBA_BLOCK and REL_BUCKETS == 32 and REL_MAX_DIST == 128

    bias_dist = rel_bias[_t5_bucket(jnp.arange(s, dtype=jnp.int32))].T
    rel = bias_dist - rel_bias[REL_BUCKETS - 1][:, None]
    a_diag, a_prev = _bias_tiles(rel[:DIFF_HEADS], ta)
    d_diag, d_prev = _bias_tiles(rel[DIFF_HEADS:], MOBA_BLOCK)
    pair = lambda a: a.reshape(MOBA_HEADS // 2, 2 * MOBA_BLOCK, MOBA_BLOCK)
    d_diag, d_prev = pair(d_diag), pair(d_prev)

    mod = _ada_mod(c, w_ada, b_ada)
    w_router_t = w_router.T.astype(BF16)
    n_slots = (-(-t * TOP_K // SLOT_BLOCK) + N_EXPERTS) * SLOT_BLOCK

    for layer in range(DEPTH):
        sh1, sc1, g1, sh2, sc2, g2 = (m[:, None, :] for m in jnp.split(mod[layer], 6, axis=-1))
        lam_init = 0.8 - 0.6 * math.exp(-0.3 * layer)
        lams = jnp.stack([lam_q1[layer], lam_k1[layer], lam_q2[layer], lam_k2[layer]])

        h, qkv_a, zpuv, qkv_d = _inproj(x, sc1, sh1, w_in[layer].astype(BF16))
        o_b, o_c = _branches(zpuv, w_pool[layer], pool_scale[layer], sgu_ln_g[layer], sgu_ln_b[layer],
                             w_spatial[layer], b_spatial[layer])
        o_a = _diff_attention(qkv_a, lams, diff_norm_g[layer].reshape(1, 2 * HEAD_DIM),
                              a_diag, a_prev, lam_init)
        kmean = _block_means(qkv_d).reshape(b, s // MOBA_BLOCK, BRANCH_WIDTH)
        kmean = jnp.pad(kmean, ((0, 0), (0, LANES - s // MOBA_BLOCK), (0, 0)))
        o_d = _moba_attention(qkv_d, kmean, d_diag, d_prev)

        flat = lambda a: a.reshape(t, a.shape[-1])
        merged = _merge(flat(h), [flat(o_a), flat(o_b), flat(o_c), flat(o_d)],
                        w_gate[layer].astype(BF16), b_gate[layer][:, None, :], w_branch[layer].astype(BF16))
        x1, h2, ridx, rwts = _outproj(merged, x, w_out[layer].astype(BF16), g1, ln1_g[layer], ln1_b[layer],
                                      sc2, sh2, w_router_t, b_router)

        e_tk = jnp.transpose(ridx[:, :TOP_K, :], (0, 2, 1)).reshape(t * TOP_K)
        wts = jnp.transpose(rwts[:, :TOP_K, :], (0, 2, 1)).reshape(t, TOP_K)
        dest, blk_expert, n_used = _slot_plan(e_tk, n_slots)
        xs = _dispatch(dest, h2.reshape(t, d), n_slots)
        ys = _expert_ffn(blk_expert, n_used, xs, w_e_gate[layer].astype(BF16),
                         w_e_up[layer].astype(BF16), w_e_down[layer].astype(BF16))
        x = _combine(dest, wts, x1, g2, ln2_g[layer], ln2_b[layer], ys)
    return x
```

```python
import functools
import math

import jax
import jax.numpy as jnp
from jax import lax
from jax.experimental import pallas as pl
from jax.experimental.pallas import tpu as pltpu

F32 = jnp.float32
BF16 = jnp.bfloat16

D_MODEL = 2048
DEPTH = 2
BRANCH_WIDTH = 512
HEAD_DIM = 64
N_SECTIONS = 9
DIFF_HEADS = 4
POOL_WINDOWS = (2, 4, 8, 16)
POOL_MAX_WINDOW = 16
SGU_CHUNK = 128
SGU_GROUPS = 4
MOBA_HEADS = 8
MOBA_BLOCK = 256
MOBA_TOPK = 3
REL_BUCKETS = 32
REL_MAX_DIST = 128
N_EXPERTS = 16
N_EXPERT_GROUPS = 4
EXPERTS_PER_GROUP = 4
TOP_K = 2
EXPERT_FF = 1408
ALPHA = (2 * DEPTH) ** 0.25
NORM_EPS = 1e-5

LANES = 128
MASK_VALUE = -1e30
ATTN_SUPER = 1024
BIAS_TILE = 256
ROW_CHUNK = 512
SLOT_BLOCK = 512
VMEM_LIMIT = 56 * 2 ** 20


def _cparams(semantics, vmem=VMEM_LIMIT):
    return pltpu.CompilerParams(dimension_semantics=semantics, vmem_limit_bytes=vmem)


def _sigmoid(x):
    return 1.0 / (1.0 + jnp.exp(-x))


def _layer_norm(r, g, b):
    mu = jnp.mean(r, -1, keepdims=True)
    var = jnp.mean(jnp.square(r - mu), -1, keepdims=True)
    return (r - mu) * lax.rsqrt(var + NORM_EPS) * g + b


def _ada_kernel(ct_ref, w_ref, b_ref, o_ref, *, batch):
    ct = ct_ref[...]
    ca = ct * _sigmoid(ct)
    w = w_ref[0]
    rows = [jnp.sum(w * ca[:, b:b + 1], axis=0, keepdims=True) for b in range(batch)]
    o_ref[0] = jnp.concatenate(rows, axis=0) + b_ref[0]


def _ada_mod(c, w_ada, b_ada):
    batch, d = c.shape
    depth, _, n = w_ada.shape
    tn = 512
    return pl.pallas_call(
        functools.partial(_ada_kernel, batch=batch),
        grid=(depth, n // tn),
        in_specs=[
            pl.BlockSpec((d, batch), lambda l, j: (0, 0)),
            pl.BlockSpec((1, d, tn), lambda l, j: (l, 0, j)),
            pl.BlockSpec((1, 1, tn), lambda l, j: (l, 0, j)),
        ],
        out_specs=pl.BlockSpec((1, batch, tn), lambda l, j: (l, 0, j)),
        out_shape=jax.ShapeDtypeStruct((depth, batch, n), F32),
        compiler_params=_cparams(("parallel", "parallel")),
        name="ada_mod",
    )(c.T, w_ada, b_ada.reshape(depth, 1, n))


def _inproj_kernel(x_ref, sc_ref, sh_ref, w_ref, h_ref, a_ref, p_ref, d_ref):
    w = BRANCH_WIDTH
    h = (x_ref[0] * (1.0 + sc_ref[0]) + sh_ref[0]).astype(BF16)
    h_ref[0] = h
    for s in range(N_SECTIONS):
        z = jnp.dot(h, w_ref[:, s * w:(s + 1) * w], preferred_element_type=F32)
        if s in (0, 6):
            z = z * (HEAD_DIM ** -0.5)
        if s < 3:
            a_ref[0, :, s * w:(s + 1) * w] = z.astype(BF16)
        elif s < 6:
            p_ref[0, :, (s - 3) * w:(s - 2) * w] = z
        else:
            d_ref[0, :, (s - 6) * w:(s - 5) * w] = z.astype(BF16)


def _inproj(x, sc, sh, w_in):
    b, s, d = x.shape
    tm = min(512, s)
    n3 = 3 * BRANCH_WIDTH
    row = lambda bi, i: (bi, i, 0)
    vec = lambda bi, i: (bi, 0, 0)
    return pl.pallas_call(
        _inproj_kernel,
        grid=(b, s // tm),
        in_specs=[
            pl.BlockSpec((1, tm, d), row),
            pl.BlockSpec((1, 1, d), vec),
            pl.BlockSpec((1, 1, d), vec),
            pl.BlockSpec(w_in.shape, lambda bi, i: (0, 0)),
        ],
        out_specs=[
            pl.BlockSpec((1, tm, d), row),
            pl.BlockSpec((1, tm, n3), row),
            pl.BlockSpec((1, tm, n3), row),
            pl.BlockSpec((1, tm, n3), row),
        ],
        out_shape=[
            jax.ShapeDtypeStruct((b, s, d), BF16),
            jax.ShapeDtypeStruct((b, s, n3), BF16),
            jax.ShapeDtypeStruct((b, s, n3), F32),
            jax.ShapeDtypeStruct((b, s, n3), BF16),
        ],
        compiler_params=_cparams(("parallel", "parallel")),
        name="inproj",
    )(x, sc, sh, w_in)


def _branches_kernel(z_ref, halo_ref, wp_ref, ps_ref, lg_ref, lb_ref, wsp_ref, bsp_ref,
                     ob_ref, oc_ref, *, tm):
    i = pl.program_id(1)
    w = BRANCH_WIDTH
    gd = w // len(POOL_WINDOWS)
    hw = POOL_MAX_WINDOW

    zp = z_ref[0, :, 0:w]
    halo = jnp.where(i > 0, halo_ref[0, :, 0:w], 0.0)
    ext = jnp.concatenate([halo, zp], axis=0)
    pos = (i * tm + lax.broadcasted_iota(jnp.int32, (tm, 1), 0) + 1).astype(F32)
    outs = []
    for g, win in enumerate(POOL_WINDOWS):
        e = ext[:, g * gd:(g + 1) * gd]
        shift = 1
        while shift < win:
            e = e + pltpu.roll(e, shift, axis=0)
            shift *= 2
        cnt = jnp.minimum(pos, float(win))
        pooled = e[hw:, :] / cnt - zp[:, g * gd:(g + 1) * gd]
        outs.append(jnp.dot(pooled.astype(BF16), wp_ref[g].astype(BF16), preferred_element_type=F32))
    ob_ref[0] = (jnp.concatenate(outs, axis=1) * ps_ref[...]).astype(BF16)

    u = jax.nn.gelu(z_ref[0, :, w:2 * w], approximate=True)
    v = _layer_norm(jax.nn.gelu(z_ref[0, :, 2 * w:3 * w], approximate=True), lg_ref[...], lb_ref[...])
    r = lax.broadcasted_iota(jnp.int32, (SGU_CHUNK, SGU_CHUNK), 0)
    c = lax.broadcasted_iota(jnp.int32, (SGU_CHUNK, SGU_CHUNK), 1)
    sg = w // SGU_GROUPS
    for g in range(SGU_GROUPS):
        wc = jnp.where(r >= c, wsp_ref[g], 0.0).astype(BF16)
        for ch in range(tm // SGU_CHUNK):
            rows = slice(ch * SGU_CHUNK, (ch + 1) * SGU_CHUNK)
            cols = slice(g * sg, (g + 1) * sg)
            vm = jnp.dot(wc, v[rows, cols].astype(BF16), preferred_element_type=F32) + bsp_ref[g]
            oc_ref[0, rows, cols] = (u[rows, cols] * vm).astype(BF16)


def _branches(zpuv, w_pool, pool_scale, ln_g, ln_b, w_sp, b_sp):
    b, s, n3 = zpuv.shape
    tm = min(512, s)
    w = BRANCH_WIDTH
    hw = POOL_MAX_WINDOW
    bsp = jnp.broadcast_to(b_sp[:, :, None], (SGU_GROUPS, SGU_CHUNK, w // SGU_GROUPS))
    full = lambda a: pl.BlockSpec(a.shape, lambda bi, i: (0,) * a.ndim)
    ps, lg, lb = pool_scale.reshape(1, w), ln_g.reshape(1, w), ln_b.reshape(1, w)
    return pl.pallas_call(
        functools.partial(_branches_kernel, tm=tm),
        grid=(b, s // tm),
        in_specs=[
            pl.BlockSpec((1, tm, n3), lambda bi, i: (bi, i, 0)),
            pl.BlockSpec((1, hw, n3), lambda bi, i: (bi, jnp.maximum(i * (tm // hw) - 1, 0), 0)),
            full(w_pool), full(ps), full(lg), full(lb), full(w_sp), full(bsp),
        ],
        out_specs=[pl.BlockSpec((1, tm, w), lambda bi, i: (bi, i, 0))] * 2,
        out_shape=[jax.ShapeDtypeStruct((b, s, w), BF16)] * 2,
        compiler_params=_cparams(("parallel", "parallel")),
        name="pool_sgu",
    )(zpuv, zpuv, w_pool, ps, lg, lb, w_sp, bsp)


_NT = (((1,), (1,)), ((), ()))


def _flash_step(qa, ka, vb, s_ref, m_sc, acc_sc, patch):
    lane = lax.broadcasted_iota(jnp.int32, vb.shape, 1)
    va = jnp.concatenate([vb, jnp.where(lane == 0, 1.0, 0.0).astype(BF16)], axis=1)
    for c0 in range(0, qa.shape[0], ROW_CHUNK):
        rows = slice(c0, c0 + ROW_CHUNK)
        s = lax.dot_general(qa[rows], ka, _NT, preferred_element_type=F32)
        if patch is not None:
            s_ref[rows, :] = s
            patch(c0)
            s = s_ref[rows, :]
        m_old = m_sc[rows]
        m_new = jnp.maximum(m_old, jnp.max(s, axis=-1, keepdims=True))
        p = jnp.exp(s - m_new).astype(BF16)
        acc_sc[rows] = jnp.exp(m_old - m_new) * acc_sc[rows] + jnp.dot(p, va, preferred_element_type=F32)
        m_sc[rows] = m_new


def _near_patches(s_ref, bias_diag, bias_prev, tq, diagonal, c0):
    t = BIAS_TILE
    n_sub = tq // t
    hh = c0 // tq
    cols = lambda a: slice(a * t, (a + 1) * t)
    for a in range((c0 % tq) // t, (c0 % tq + ROW_CHUNK) // t):
        rows = slice(hh * tq + a * t, hh * tq + (a + 1) * t)
        if not diagonal:
            if a == 0:
                s_ref[rows, cols(n_sub - 1)] = s_ref[rows, cols(n_sub - 1)] + bias_prev(hh)
            continue
        s_ref[rows, cols(a)] = s_ref[rows, cols(a)] + bias_diag(hh)
        if a >= 1:
            s_ref[rows, cols(a - 1)] = s_ref[rows, cols(a - 1)] + bias_prev(hh)
        if a + 1 < n_sub:
            s_ref[rows, (a + 1) * t:tq] = jnp.full((t, tq - (a + 1) * t), MASK_VALUE, F32)


def _run_causal_tiles(i, step):
    def far_step(j, carry):
        step(j, None)
        return carry

    lax.fori_loop(0, jnp.maximum(i - 1, 0), far_step, 0)

    @pl.when(i >= 1)
    def _():
        step(i - 1, False)

    step(i, True)


def _attn_scratch(tq):
    return [pltpu.VMEM((2 * tq, tq), F32), pltpu.VMEM((2 * tq, 1), F32), pltpu.VMEM((2 * tq, 2 * LANES), F32)]


def _diff_attn_kernel(lam_ref, g_ref, q_ref, k_ref, v_ref, bd_ref, bp_ref, o_ref,
                      s_ref, m_sc, acc_sc, *, tq, lam_init):
    i = pl.program_id(2)
    q = q_ref[0]
    lane = lax.broadcasted_iota(jnp.int32, (1, LANES), 1)
    zero = jnp.zeros_like(q)
    qq = jnp.concatenate([jnp.where(lane < HEAD_DIM, q, zero),
                          jnp.where(lane >= HEAD_DIM, q, zero)], axis=0)
    m_sc[...] = jnp.full(m_sc.shape, MASK_VALUE, F32)
    acc_sc[...] = jnp.zeros(acc_sc.shape, F32)

    def step(j, diagonal):
        start = pl.multiple_of(j * tq, tq)
        patch = None if diagonal is None else functools.partial(
            _near_patches, s_ref, lambda hh: bd_ref[0], lambda hh: bp_ref[0], tq, diagonal)
        _flash_step(qq, k_ref[0, pl.ds(start, tq), :], v_ref[0, pl.ds(start, tq), :],
                    s_ref, m_sc, acc_sc, patch)

    _run_causal_tiles(i, step)

    acc = acc_sc[...]
    o = acc[:, :LANES] / acc[:, LANES:LANES + 1]
    lam = (jnp.exp(jnp.sum(lam_ref[0:1, :] * lam_ref[1:2, :], axis=-1, keepdims=True))
           - jnp.exp(jnp.sum(lam_ref[2:3, :] * lam_ref[3:4, :], axis=-1, keepdims=True)) + lam_init)
    od = o[:tq] - lam * o[tq:]
    ms = jnp.mean(jnp.square(od), -1, keepdims=True)
    o_ref[0] = ((od * lax.rsqrt(ms + NORM_EPS) * g_ref[...]) * (1.0 - lam_init)).astype(BF16)


def _diff_attention(qkv, lams, norm_g, bias_diag, bias_prev, lam_init):
    b, s, _ = qkv.shape
    tq = min(ATTN_SUPER, s)
    t = BIAS_TILE
    nh = DIFF_HEADS
    return pl.pallas_call(
        functools.partial(_diff_attn_kernel, tq=tq, lam_init=lam_init),
        grid=(b, nh, s // tq),
        in_specs=[
            pl.BlockSpec(lams.shape, lambda bi, h, i: (0, 0)),
            pl.BlockSpec(norm_g.shape, lambda bi, h, i: (0, 0)),
            pl.BlockSpec((1, tq, LANES), lambda bi, h, i: (bi, i, h)),
            pl.BlockSpec((1, s, LANES), lambda bi, h, i: (bi, 0, nh + h)),
            pl.BlockSpec((1, s, LANES), lambda bi, h, i: (bi, 0, 2 * nh + h)),
            pl.BlockSpec((1, t, t), lambda bi, h, i: (h, 0, 0)),
            pl.BlockSpec((1, t, t), lambda bi, h, i: (h, 0, 0)),
        ],
        out_specs=pl.BlockSpec((1, tq, LANES), lambda bi, h, i: (bi, i, h)),
        out_shape=jax.ShapeDtypeStruct((b, s, BRANCH_WIDTH), BF16),
        scratch_shapes=_attn_scratch(tq),
        compiler_params=_cparams(("parallel", "parallel", "arbitrary")),
        name="diff_attention",
    )(lams, norm_g, qkv, qkv, qkv, bias_diag, bias_prev)


def _kmean_kernel(k_ref, o_ref):
    o_ref[0, 0] = jnp.mean(k_ref[0].astype(F32), axis=0, keepdims=True)


def _block_means(qkv):
    b, s, _ = qkv.shape
    nb = s // MOBA_BLOCK
    w = BRANCH_WIDTH
    return pl.pallas_call(
        _kmean_kernel,
        grid=(b, nb),
        in_specs=[pl.BlockSpec((1, MOBA_BLOCK, w), lambda bi, j: (bi, j, 1))],
        out_specs=pl.BlockSpec((1, 1, 1, w), lambda bi, j: (bi, j, 0, 0)),
        out_shape=jax.ShapeDtypeStruct((b, nb, 1, w), F32),
        compiler_params=_cparams(("parallel", "parallel")),
        name="moba_kmean",
    )(qkv)


def _moba_kernel(q_ref, km_ref, k_ref, v_ref, bd_ref, bp_ref, o_ref,
                 qa_sc, s_ref, m_sc, acc_sc, *, tq):
    i = pl.program_id(2)
    shift = int(math.log2(MOBA_BLOCK))
    q = q_ref[0]
    lane = lax.broadcasted_iota(jnp.int32, (1, LANES), 1)
    blk = lax.broadcasted_iota(jnp.int32, (tq, LANES), 1).astype(F32)
    row = lax.broadcasted_iota(jnp.int32, (tq, 1), 0)
    own = lax.shift_right_logical(i * tq + row, shift).astype(F32)
    zero = jnp.zeros_like(q)
    km = km_ref[0].astype(BF16)
    ninf = jnp.float32(-jnp.inf)
    for hh in range(2):
        qh = jnp.where((lane >= hh * HEAD_DIM) & (lane < (hh + 1) * HEAD_DIM), q, zero)
        gate = lax.dot_general(qh, km, _NT, preferred_element_type=F32)
        g = jnp.where(blk < own, gate, ninf)
        sel = blk >= own
        for _ in range(MOBA_TOPK):
            mx = jnp.max(g, axis=-1, keepdims=True)
            first = jnp.min(jnp.where(g == mx, blk, float(LANES)), axis=-1, keepdims=True)
            hit = blk == first
            sel = sel | (hit & (mx > ninf))
            g = jnp.where(hit, ninf, g)
        qa_sc[hh * tq:(hh + 1) * tq, :] = jnp.concatenate(
            [qh, jnp.where(sel, 0.0, MASK_VALUE).astype(BF16)], axis=1)
    m_sc[...] = jnp.full(m_sc.shape, MASK_VALUE, F32)
    acc_sc[...] = jnp.zeros(acc_sc.shape, F32)

    def step(j, diagonal):
        start = pl.multiple_of(j * tq, tq)
        kb = k_ref[0, pl.ds(start, tq), :]
        kblk = lax.shift_right_logical(start + row, shift)
        ka = jnp.concatenate([kb, jnp.where(lane == kblk, 1.0, 0.0).astype(BF16)], axis=1)
        patch = None if diagonal is None else functools.partial(
            _near_patches, s_ref, lambda hh: bd_ref[0, hh], lambda hh: bp_ref[0, hh], tq, diagonal)
        _flash_step(qa_sc, ka, v_ref[0, pl.ds(start, tq), :], s_ref, m_sc, acc_sc, patch)

    _run_causal_tiles(i, step)

    acc = acc_sc[...]
    o = acc[:, :LANES] / acc[:, LANES:LANES + 1]
    o_ref[0] = jnp.where(lane < HEAD_DIM, o[:tq], o[tq:]).astype(BF16)


def _moba_attention(qkv, kmean, bias_diag, bias_prev):
    b, s, _ = qkv.shape
    tq = min(ATTN_SUPER, s)
    t = BIAS_TILE
    npair = MOBA_HEADS // 2
    return pl.pallas_call(
        functools.partial(_moba_kernel, tq=tq),
        grid=(b, npair, s // tq),
        in_specs=[
            pl.BlockSpec((1, tq, LANES), lambda bi, p, i: (bi, i, p)),
            pl.BlockSpec((1, LANES, LANES), lambda bi, p, i: (bi, 0, p)),
            pl.BlockSpec((1, s, LANES), lambda bi, p, i: (bi, 0, npair + p)),
            pl.BlockSpec((1, s, LANES), lambda bi, p, i: (bi, 0, 2 * npair + p)),
            pl.BlockSpec((1, 2, t, t), lambda bi, p, i: (p, 0, 0, 0)),
            pl.BlockSpec((1, 2, t, t), lambda bi, p, i: (p, 0, 0, 0)),
        ],
        out_specs=pl.BlockSpec((1, tq, LANES), lambda bi, p, i: (bi, i, p)),
        out_shape=jax.ShapeDtypeStruct((b, s, BRANCH_WIDTH), BF16),
        scratch_shapes=[pltpu.VMEM((2 * tq, 2 * LANES), BF16)] + _attn_scratch(tq),
        compiler_params=_cparams(("parallel", "parallel", "arbitrary")),
        name="moba_attention",
    )(qkv, kmean, qkv, qkv, bias_diag, bias_prev)


def _merge_kernel(h_ref, oa_ref, ob_ref, oc_ref, od_ref, wg_ref, bg_ref, wb_ref, out_ref):
    h = h_ref[...]
    acc = None
    for br, o_ref in enumerate((oa_ref, ob_ref, oc_ref, od_ref)):
        gate = _sigmoid(jnp.dot(h, wg_ref[br], preferred_element_type=F32) + bg_ref[br])
        term = gate * jnp.dot(o_ref[...], wb_ref[br], preferred_element_type=F32)
        acc = term if acc is None else acc + term
    out_ref[...] = acc.astype(BF16)


def _merge(h, branches, w_gate, b_gate, w_branch):
    t, d = h.shape
    w = BRANCH_WIDTH
    tm = min(512, t)
    tn = 512
    nbr = len(branches)
    row = lambda j, i: (i, 0)
    return pl.pallas_call(
        _merge_kernel,
        grid=(d // tn, t // tm),
        in_specs=[pl.BlockSpec((tm, d), row)] + [pl.BlockSpec((tm, w), row)] * nbr + [
            pl.BlockSpec((nbr, d, tn), lambda j, i: (0, 0, j)),
            pl.BlockSpec((nbr, 1, tn), lambda j, i: (0, 0, j)),
            pl.BlockSpec((nbr, w, tn), lambda j, i: (0, 0, j)),
        ],
        out_specs=pl.BlockSpec((tm, tn), lambda j, i: (i, j)),
        out_shape=jax.ShapeDtypeStruct((t, d), BF16),
        compiler_params=_cparams(("parallel", "parallel")),
        name="gated_merge",
    )(h, *branches, w_gate, b_gate, w_branch)


def _route(aff, sel):
    ng, pg = N_EXPERT_GROUPS, EXPERTS_PER_GROUP
    srow = [sel[e:e + 1] for e in range(N_EXPERTS)]
    arow = [aff[e:e + 1] for e in range(N_EXPERTS)]
    best_score, best_grp = None, None
    for g in range(ng):
        rows = srow[g * pg:(g + 1) * pg]
        score = None
        for a in range(pg):
            for b in range(a + 1, pg):
                pair = rows[a] + rows[b]
                score = pair if score is None else jnp.maximum(score, pair)
        if best_score is None:
            best_score, best_grp = score, jnp.zeros_like(score, dtype=jnp.int32)
        else:
            better = score > best_score
            best_score = jnp.where(better, score, best_score)
            best_grp = jnp.where(better, g, best_grp)

    def pick(rows_by_group):
        out = rows_by_group[0]
        for g in range(1, ng):
            out = jnp.where(best_grp == g, rows_by_group[g], out)
        return out

    cand_s = [pick([srow[g * pg + k] for g in range(ng)]) for k in range(pg)]
    cand_a = [pick([arow[g * pg + k] for g in range(ng)]) for k in range(pg)]

    def argmax_first(vals, exclude):
        bv, bi, ba = None, None, None
        for k in range(pg):
            v = vals[k] if exclude is None else jnp.where(exclude == k, -jnp.inf, vals[k])
            if bv is None:
                bv, bi, ba = v, jnp.zeros_like(best_grp), cand_a[k]
            else:
                better = v > bv
                bv = jnp.where(better, v, bv)
                bi = jnp.where(better, k, bi)
                ba = jnp.where(better, cand_a[k], ba)
        return bi, ba

    i0, a0 = argmax_first(cand_s, None)
    i1, a1 = argmax_first(cand_s, i0)
    tot = a0 + a1
    return best_grp * pg + i0, best_grp * pg + i1, a0 / tot, a1 / tot


def _outproj_kernel(m_ref, x_ref, wo_ref, g1_ref, lg_ref, lb_ref, sc2_ref, sh2_ref, wr_ref, br_ref,
                    x1_ref, h2_ref, ri_ref, rw_ref, *, tm):
    y = jnp.dot(m_ref[0], wo_ref[...], preferred_element_type=F32)
    x1 = _layer_norm(ALPHA * x_ref[0] + (1.0 + g1_ref[0]) * y, lg_ref[...], lb_ref[...])
    x1_ref[0] = x1
    h2 = x1 * (1.0 + sc2_ref[0]) + sh2_ref[0]
    h2_ref[0] = h2
    logits = lax.dot_general(wr_ref[...], h2.astype(BF16), _NT, preferred_element_type=F32)
    aff = _sigmoid(logits)
    e0, e1, w0, w1 = _route(aff, aff + br_ref[...])
    ri_ref[0] = jnp.concatenate([e0, e1, jnp.zeros((6, tm), jnp.int32)], axis=0)
    rw_ref[0] = jnp.concatenate([w0, w1, jnp.zeros((6, tm), F32)], axis=0)


def _outproj(merged, x, w_out, g1, ln_g, ln_b, sc2, sh2, w_router_t, b_router):
    b, s, d = x.shape
    tm = min(512, s)
    row = lambda bi, i: (bi, i, 0)
    vec = lambda bi, i: (bi, 0, 0)
    full = lambda a: pl.BlockSpec(a.shape, lambda bi, i: (0,) * a.ndim)
    lg, lb = ln_g.reshape(1, d), ln_b.reshape(1, d)
    br = b_router.reshape(N_EXPERTS, 1)
    return pl.pallas_call(
        functools.partial(_outproj_kernel, tm=tm),
        grid=(b, s // tm),
        in_specs=[
            pl.BlockSpec((1, tm, d), row), pl.BlockSpec((1, tm, d), row), full(w_out),
            pl.BlockSpec((1, 1, d), vec), full(lg), full(lb),
            pl.BlockSpec((1, 1, d), vec), pl.BlockSpec((1, 1, d), vec),
            full(w_router_t), full(br),
        ],
        out_specs=[
            pl.BlockSpec((1, tm, d), row), pl.BlockSpec((1, tm, d), row),
            pl.BlockSpec((1, 8, tm), lambda bi, i: (bi, 0, i)),
            pl.BlockSpec((1, 8, tm), lambda bi, i: (bi, 0, i)),
        ],
        out_shape=[
            jax.ShapeDtypeStruct((b, s, d), F32), jax.ShapeDtypeStruct((b, s, d), F32),
            jax.ShapeDtypeStruct((b, 8, s), jnp.int32), jax.ShapeDtypeStruct((b, 8, s), F32),
        ],
        compiler_params=_cparams(("parallel", "parallel")),
        name="outproj_ln_router",
    )(merged.reshape(b, s, d), x, w_out, g1, lg, lb, sc2, sh2, w_router_t, br)


def _row_copy(src, src_row, dst, dst_row, sem):
    return pltpu.make_async_copy(src.at[pl.ds(src_row, 1)], dst.at[pl.ds(dst_row, 1)], sem)


def _dispatch_kernel(dest_ref, h2_ref, xs_in_ref, xs_ref, sem, *, rows):
    del xs_in_ref

    def issue(r, carry):
        for k in range(TOP_K):
            _row_copy(h2_ref, r, xs_ref, dest_ref[0, 0, TOP_K * r + k], sem).start()
        return carry

    def drain(r, carry):
        for k in range(TOP_K):
            _row_copy(h2_ref, 0, xs_ref, 0, sem).wait()
        return carry

    lax.fori_loop(0, rows, issue, 0)
    lax.fori_loop(0, rows, drain, 0)


def _dispatch(dest, h2, n_slots):
    t, d = h2.shape
    rows = min(512, t)
    return pl.pallas_call(
        functools.partial(_dispatch_kernel, rows=rows),
        grid=(t // rows,),
        in_specs=[
            pl.BlockSpec((1, 1, TOP_K * rows), lambda i: (i, 0, 0), memory_space=pltpu.SMEM),
            pl.BlockSpec((rows, d), lambda i: (i, 0)),
            pl.BlockSpec(memory_space=pl.ANY),
        ],
        out_specs=pl.BlockSpec(memory_space=pl.ANY),
        out_shape=jax.ShapeDtypeStruct((n_slots, d), F32),
        scratch_shapes=[pltpu.SemaphoreType.DMA(())],
        input_output_aliases={2: 0},
        compiler_params=_cparams(("arbitrary",)),
        name="moe_dispatch",
    )(dest.reshape(t // rows, 1, TOP_K * rows), h2, jnp.zeros((n_slots, d), F32))


def _expert_kernel(be_ref, nu_ref, xs_ref, wg_ref, wu_ref, wd_ref, ys_ref):
    del be_ref
    used = pl.program_id(0) < nu_ref[0]

    @pl.when(used)
    def _():
        xb = xs_ref[...].astype(BF16)
        g = jnp.dot(xb, wg_ref[0], preferred_element_type=F32)
        u = jnp.dot(xb, wu_ref[0], preferred_element_type=F32)
        act = ((g * _sigmoid(g)) * u).astype(BF16)
        ys_ref[...] = jnp.dot(act, wd_ref[0], preferred_element_type=F32)

    @pl.when(jnp.logical_not(used))
    def _():
        ys_ref[...] = jnp.zeros(ys_ref.shape, F32)


def _expert_ffn(blk_expert, n_used, xs, w_g, w_u, w_d):
    n_slots, d = xs.shape
    f = w_g.shape[-1]
    blk = SLOT_BLOCK
    last = lambda b, nu: jnp.minimum(b, nu[0] - 1)
    wspec = lambda shape: pl.BlockSpec(shape, lambda b, be, nu: (be[last(b, nu)], 0, 0),
                                       pipeline_mode=pl.Buffered(1))
    return pl.pallas_call(
        _expert_kernel,
        grid_spec=pltpu.PrefetchScalarGridSpec(
            num_scalar_prefetch=2,
            grid=(n_slots // blk,),
            in_specs=[
                pl.BlockSpec((blk, d), lambda b, be, nu: (last(b, nu), 0)),
                wspec((1, d, f)), wspec((1, d, f)), wspec((1, f, d)),
            ],
            out_specs=pl.BlockSpec((blk, d), lambda b, be, nu: (b, 0)),
        ),
        out_shape=jax.ShapeDtypeStruct((n_slots, d), F32),
        compiler_params=_cparams(("arbitrary",)),
        name="moe_experts",
    )(blk_expert, n_used, xs, w_g, w_u, w_d)


def _combine_kernel(dest_ref, w_ref, x1_ref, g2_ref, lg_ref, lb_ref, ys_ref, o_ref, buf, sem, *, rows):
    def issue(r, carry):
        for k in range(TOP_K):
            _row_copy(ys_ref, dest_ref[0, 0, TOP_K * r + k], buf.at[k], r, sem).start()
        return carry

    def drain(r, carry):
        for k in range(TOP_K):
            _row_copy(ys_ref, 0, buf.at[k], 0, sem).wait()
        return carry

    lax.fori_loop(0, rows, issue, 0)
    lax.fori_loop(0, rows, drain, 0)
    y = buf[0] * w_ref[:, 0:1] + buf[1] * w_ref[:, 1:2]
    o_ref[0] = _layer_norm(ALPHA * x1_ref[0] + (1.0 + g2_ref[0]) * y, lg_ref[...], lb_ref[...])


def _combine(dest, wts, x1, g2, ln_g, ln_b, ys):
    b, s, d = x1.shape
    rows = min(256, s)
    nt = s // rows
    lg, lb = ln_g.reshape(1, d), ln_b.reshape(1, d)
    full = lambda a: pl.BlockSpec(a.shape, lambda bi, i: (0,) * a.ndim)
    return pl.pallas_call(
        functools.partial(_combine_kernel, rows=rows),
        grid=(b, nt),
        in_specs=[
            pl.BlockSpec((1, 1, TOP_K * rows), lambda bi, i: (bi * nt + i, 0, 0), memory_space=pltpu.SMEM),
            pl.BlockSpec((rows, TOP_K), lambda bi, i: (bi * nt + i, 0)),
            pl.BlockSpec((1, rows, d), lambda bi, i: (bi, i, 0)),
            pl.BlockSpec((1, 1, d), lambda bi, i: (bi, 0, 0)),
            full(lg), full(lb),
            pl.BlockSpec(memory_space=pl.ANY),
        ],
        out_specs=pl.BlockSpec((1, rows, d), lambda bi, i: (bi, i, 0)),
        out_shape=jax.ShapeDtypeStruct((b, s, d), F32),
        scratch_shapes=[pltpu.VMEM((TOP_K, rows, d), F32), pltpu.SemaphoreType.DMA(())],
        compiler_params=_cparams(("arbitrary", "arbitrary")),
        name="moe_combine_ln",
    )(dest.reshape(b * nt, 1, TOP_K * rows), wts, x1, g2, lg, lb, ys)


def _slot_plan(e_flat, n_slots):
    blk = SLOT_BLOCK
    onehot = (e_flat[:, None] == jnp.arange(N_EXPERTS, dtype=jnp.int32)[None, :]).astype(jnp.int32)
    csum = jnp.cumsum(onehot, axis=0)
    rank = jnp.sum(csum * onehot, axis=1) - 1
    counts = csum[-1]
    padded = (counts + blk - 1) // blk * blk
    pend = jnp.cumsum(padded)
    dest = (pend - padded)[e_flat] + rank
    nblk = n_slots // blk
    blk_expert = jnp.minimum(
        jnp.searchsorted(pend, jnp.arange(nblk, dtype=jnp.int32) * blk, side='right'), N_EXPERTS - 1)
    return dest.astype(jnp.int32), blk_expert.astype(jnp.int32), (pend[-1:] // blk).astype(jnp.int32)


def _t5_bucket(dist):
    max_exact = REL_BUCKETS // 2
    d = jnp.maximum(dist, 1).astype(F32)
    large = max_exact + (jnp.log(d / max_exact) / math.log(REL_MAX_DIST / max_exact)
                         * (REL_BUCKETS - max_exact)).astype(jnp.int32)
    large = jnp.minimum(large, REL_BUCKETS - 1)
    return jnp.where(dist < max_exact, dist, large)


def _bias_tiles(rel_table, t):
    r = jnp.arange(t, dtype=jnp.int32)[:, None]
    c = jnp.arange(t, dtype=jnp.int32)[None, :]
    dd = r - c

    def lookup(dist):
        onehot = (_t5_bucket(dist)[:, :, None] == jnp.arange(REL_BUCKETS, dtype=jnp.int32)).astype(F32)
        return jnp.einsum('rck,kh->hrc', onehot, rel_table, precision=lax.Precision.HIGHEST)

    diag = jnp.where(dd >= 0, lookup(jnp.maximum(dd, 0)), MASK_VALUE)
    return diag, lookup(dd + t)


def kernel(x, c, rel_bias, w_router, b_router, w_ada, b_ada, w_in, lam_q1, lam_k1, lam_q2, lam_k2,
           diff_norm_g, w_pool, pool_scale, sgu_ln_g, sgu_ln_b, w_spatial, b_spatial, w_branch,
           w_gate, b_gate, w_out, ln1_g, ln1_b, w_e_gate, w_e_up, w_e_down, ln2_g, ln2_b):
    b, s, d = x.shape
    t = b * s
    assert d == D_MODEL and s % MOBA_BLOCK == 0 and s // MOBA_BLOCK <= LANES
    assert int(16 * 8 ** (15 / 16)) + 1 < MOBA_BLOCK and REL_BUCKETS == 32 and REL_MAX_DIST == 128

    assert s % min(ATTN_SUPER, s) == 0 and BIAS_TILE == MOBA_BLOCK
    diag, prev = _bias_tiles(rel_bias - rel_bias[REL_BUCKETS - 1][None, :], BIAS_TILE)
    a_diag, a_prev = diag[:DIFF_HEADS], prev[:DIFF_HEADS]
    pair = lambda a: a[DIFF_HEADS:].reshape(MOBA_HEADS // 2, 2, BIAS_TILE, BIAS_TILE)
    d_diag, d_prev = pair(diag), pair(prev)

    mod = _ada_mod(c, w_ada, b_ada)
    w_router_t = w_router.T.astype(BF16)
    n_slots = (-(-t * TOP_K // SLOT_BLOCK) + N_EXPERTS) * SLOT_BLOCK

    for layer in range(DEPTH):
        sh1, sc1, g1, sh2, sc2, g2 = (m[:, None, :] for m in jnp.split(mod[layer], 6, axis=-1))
        lam_init = 0.8 - 0.6 * math.exp(-0.3 * layer)
        lams = jnp.stack([lam_q1[layer], lam_k1[layer], lam_q2[layer], lam_k2[layer]])

        h, qkv_a, zpuv, qkv_d = _inproj(x, sc1, sh1, w_in[layer].astype(BF16))
        o_b, o_c = _branches(zpuv, w_pool[layer], pool_scale[layer], sgu_ln_g[layer], sgu_ln_b[layer],
                             w_spatial[layer], b_spatial[layer])
        o_a = _diff_attention(qkv_a, lams, diff_norm_g[layer].reshape(1, 2 * HEAD_DIM),
                              a_diag, a_prev, lam_init)
        kmean = _block_means(qkv_d).reshape(b, s // MOBA_BLOCK, BRANCH_WIDTH)
        kmean = jnp.pad(kmean, ((0, 0), (0, LANES - s // MOBA_BLOCK), (0, 0)))
        o_d = _moba_attention(qkv_d, kmean, d_diag, d_prev)

        flat = lambda a: a.reshape(t, a.shape[-1])
        merged = _merge(flat(h), [flat(o_a), flat(o_b), flat(o_c), flat(o_d)],
                        w_gate[layer].astype(BF16), b_gate[layer][:, None, :], w_branch[layer].astype(BF16))
        x1, h2, ridx, rwts = _outproj(merged, x, w_out[layer].astype(BF16), g1, ln1_g[layer], ln1_b[layer],
                                      sc2, sh2, w_router_t, b_router)

        e_tk = jnp.transpose(ridx[:, :TOP_K, :], (0, 2, 1)).reshape(t * TOP_K)
        wts = jnp.transpose(rwts[:, :TOP_K, :], (0, 2, 1)).reshape(t, TOP_K)
        dest, blk_expert, n_used = _slot_plan(e_tk, n_slots)
        xs = _dispatch(dest, h2.reshape(t, d), n_slots)
        ys = _expert_ffn(blk_expert, n_used, xs, w_e_gate[layer].astype(BF16),
                         w_e_up[layer].astype(BF16), w_e_down[layer].astype(BF16))
        x = _combine(dest, wts, x1, g2, ln2_g[layer], ln2_b[layer], ys)
    return x
```

```python
import functools
import math

import jax
import jax.numpy as jnp
from jax import lax
from jax.experimental import pallas as pl
from jax.experimental.pallas import tpu as pltpu

F32 = jnp.float32
BF16 = jnp.bfloat16

D_MODEL = 2048
DEPTH = 2
BRANCH_WIDTH = 512
HEAD_DIM = 64
N_SECTIONS = 9
DIFF_HEADS = 4
POOL_WINDOWS = (2, 4, 8, 16)
POOL_MAX_WINDOW = 16
SGU_CHUNK = 128
SGU_GROUPS = 4
MOBA_HEADS = 8
MOBA_BLOCK = 256
MOBA_TOPK = 3
REL_BUCKETS = 32
REL_MAX_DIST = 128
N_EXPERTS = 16
N_EXPERT_GROUPS = 4
EXPERTS_PER_GROUP = 4
TOP_K = 2
EXPERT_FF = 1408
ALPHA = (2 * DEPTH) ** 0.25
NORM_EPS = 1e-5

LANES = 128
MASK_VALUE = -1e30
ATTN_SUPER = 1024
BIAS_TILE = 256
ROW_CHUNK = 512
SLOT_BLOCK = 512
VMEM_LIMIT = 56 * 2 ** 20


def _cparams(semantics, vmem=VMEM_LIMIT):
    return pltpu.CompilerParams(dimension_semantics=semantics, vmem_limit_bytes=vmem)


def _sigmoid(x):
    return 1.0 / (1.0 + jnp.exp(-x))


def _layer_norm(r, g, b):
    mu = jnp.mean(r, -1, keepdims=True)
    var = jnp.mean(jnp.square(r - mu), -1, keepdims=True)
    return (r - mu) * lax.rsqrt(var + NORM_EPS) * g + b


def _ada_kernel(ct_ref, w_ref, b_ref, o_ref, *, batch):
    ct = ct_ref[...]
    ca = ct * _sigmoid(ct)
    w = w_ref[0]
    rows = [jnp.sum(w * ca[:, b:b + 1], axis=0, keepdims=True) for b in range(batch)]
    o_ref[0] = jnp.concatenate(rows, axis=0) + b_ref[0]


def _ada_mod(c, w_ada, b_ada):
    batch, d = c.shape
    depth, _, n = w_ada.shape
    tn = 512
    return pl.pallas_call(
        functools.partial(_ada_kernel, batch=batch),
        grid=(depth, n // tn),
        in_specs=[
            pl.BlockSpec((d, batch), lambda l, j: (0, 0)),
            pl.BlockSpec((1, d, tn), lambda l, j: (l, 0, j)),
            pl.BlockSpec((1, 1, tn), lambda l, j: (l, 0, j)),
        ],
        out_specs=pl.BlockSpec((1, batch, tn), lambda l, j: (l, 0, j)),
        out_shape=jax.ShapeDtypeStruct((depth, batch, n), F32),
        compiler_params=_cparams(("parallel", "parallel")),
        name="ada_mod",
    )(c.T, w_ada, b_ada.reshape(depth, 1, n))


def _inproj_kernel(x_ref, sc_ref, sh_ref, w_ref, h_ref, a_ref, p_ref, d_ref):
    w = BRANCH_WIDTH
    h = (x_ref[0] * (1.0 + sc_ref[0]) + sh_ref[0]).astype(BF16)
    h_ref[0] = h
    for s in range(N_SECTIONS):
        z = jnp.dot(h, w_ref[:, s * w:(s + 1) * w], preferred_element_type=F32)
        if s in (0, 6):
            z = z * (HEAD_DIM ** -0.5)
        if s < 3:
            a_ref[0, :, s * w:(s + 1) * w] = z.astype(BF16)
        elif s < 6:
            p_ref[0, :, (s - 3) * w:(s - 2) * w] = z
        else:
            d_ref[0, :, (s - 6) * w:(s - 5) * w] = z.astype(BF16)


def _inproj(x, sc, sh, w_in):
    b, s, d = x.shape
    tm = min(512, s)
    n3 = 3 * BRANCH_WIDTH
    row = lambda bi, i: (bi, i, 0)
    vec = lambda bi, i: (bi, 0, 0)
    return pl.pallas_call(
        _inproj_kernel,
        grid=(b, s // tm),
        in_specs=[
            pl.BlockSpec((1, tm, d), row),
            pl.BlockSpec((1, 1, d), vec),
            pl.BlockSpec((1, 1, d), vec),
            pl.BlockSpec(w_in.shape, lambda bi, i: (0, 0)),
        ],
        out_specs=[
            pl.BlockSpec((1, tm, d), row),
            pl.BlockSpec((1, tm, n3), row),
            pl.BlockSpec((1, tm, n3), row),
            pl.BlockSpec((1, tm, n3), row),
        ],
        out_shape=[
            jax.ShapeDtypeStruct((b, s, d), BF16),
            jax.ShapeDtypeStruct((b, s, n3), BF16),
            jax.ShapeDtypeStruct((b, s, n3), F32),
            jax.ShapeDtypeStruct((b, s, n3), BF16),
        ],
        compiler_params=_cparams(("parallel", "parallel")),
        name="inproj",
    )(x, sc, sh, w_in)


def _branches_kernel(z_ref, halo_ref, wp_ref, ps_ref, lg_ref, lb_ref, wsp_ref, bsp_ref,
                     ob_ref, oc_ref, *, tm):
    i = pl.program_id(1)
    w = BRANCH_WIDTH
    gd = w // len(POOL_WINDOWS)
    hw = POOL_MAX_WINDOW

    zp = z_ref[0, :, 0:w]
    halo = jnp.where(i > 0, halo_ref[0, :, 0:w], 0.0)
    ext = jnp.concatenate([halo, zp], axis=0)
    pos = (i * tm + lax.broadcasted_iota(jnp.int32, (tm, 1), 0) + 1).astype(F32)
    outs = []
    for g, win in enumerate(POOL_WINDOWS):
        e = ext[:, g * gd:(g + 1) * gd]
        shift = 1
        while shift < win:
            e = e + pltpu.roll(e, shift, axis=0)
            shift *= 2
        cnt = jnp.minimum(pos, float(win))
        pooled = e[hw:, :] / cnt - zp[:, g * gd:(g + 1) * gd]
        outs.append(jnp.dot(pooled.astype(BF16), wp_ref[g].astype(BF16), preferred_element_type=F32))
    ob_ref[0] = (jnp.concatenate(outs, axis=1) * ps_ref[...]).astype(BF16)

    u = jax.nn.gelu(z_ref[0, :, w:2 * w], approximate=True)
    v = _layer_norm(jax.nn.gelu(z_ref[0, :, 2 * w:3 * w], approximate=True), lg_ref[...], lb_ref[...])
    r = lax.broadcasted_iota(jnp.int32, (SGU_CHUNK, SGU_CHUNK), 0)
    c = lax.broadcasted_iota(jnp.int32, (SGU_CHUNK, SGU_CHUNK), 1)
    sg = w // SGU_GROUPS
    for g in range(SGU_GROUPS):
        wc = jnp.where(r >= c, wsp_ref[g], 0.0).astype(BF16)
        for ch in range(tm // SGU_CHUNK):
            rows = slice(ch * SGU_CHUNK, (ch + 1) * SGU_CHUNK)
            cols = slice(g * sg, (g + 1) * sg)
            vm = jnp.dot(wc, v[rows, cols].astype(BF16), preferred_element_type=F32) + bsp_ref[g]
            oc_ref[0, rows, cols] = (u[rows, cols] * vm).astype(BF16)


def _branches(zpuv, w_pool, pool_scale, ln_g, ln_b, w_sp, b_sp):
    b, s, n3 = zpuv.shape
    tm = min(512, s)
    w = BRANCH_WIDTH
    hw = POOL_MAX_WINDOW
    bsp = jnp.broadcast_to(b_sp[:, :, None], (SGU_GROUPS, SGU_CHUNK, w // SGU_GROUPS))
    full = lambda a: pl.BlockSpec(a.shape, lambda bi, i: (0,) * a.ndim)
    ps, lg, lb = pool_scale.reshape(1, w), ln_g.reshape(1, w), ln_b.reshape(1, w)
    return pl.pallas_call(
        functools.partial(_branches_kernel, tm=tm),
        grid=(b, s // tm),
        in_specs=[
            pl.BlockSpec((1, tm, n3), lambda bi, i: (bi, i, 0)),
            pl.BlockSpec((1, hw, n3), lambda bi, i: (bi, jnp.maximum(i * (tm // hw) - 1, 0), 0)),
            full(w_pool), full(ps), full(lg), full(lb), full(w_sp), full(bsp),
        ],
        out_specs=[pl.BlockSpec((1, tm, w), lambda bi, i: (bi, i, 0))] * 2,
        out_shape=[jax.ShapeDtypeStruct((b, s, w), BF16)] * 2,
        compiler_params=_cparams(("parallel", "parallel")),
        name="pool_sgu",
    )(zpuv, zpuv, w_pool, ps, lg, lb, w_sp, bsp)


_NT = (((1,), (1,)), ((), ()))


def _diag_keys(tq, c0):
    return c0 % tq + ROW_CHUNK


def _flash_tile(qa, ka_next, vb, s_ref, m_sc, acc_sc, patch=None, keys=None, keys_next=None):
    n_rows, tk = s_ref.shape
    if vb is not None:
        lane = lax.broadcasted_iota(jnp.int32, vb.shape, 1)
        va = jnp.concatenate([vb, jnp.where(lane == 0, 1.0, 0.0).astype(BF16)], axis=1)
    for c0 in range(0, n_rows, ROW_CHUNK):
        rows = slice(c0, c0 + ROW_CHUNK)
        if vb is not None:
            if patch is not None:
                patch(c0)
            nk = tk if keys is None else keys(c0)
            s = s_ref[rows, :nk]
            m_old = m_sc[rows]
            m_new = jnp.maximum(m_old, jnp.max(s, axis=-1, keepdims=True))
            p = jnp.exp(s - m_new).astype(BF16)
            acc_sc[rows] = (jnp.exp(m_old - m_new) * acc_sc[rows]
                            + jnp.dot(p, va[:nk], preferred_element_type=F32))
            m_sc[rows] = m_new
        if ka_next is not None:
            nk = tk if keys_next is None else keys_next(c0)
            s_ref[rows, :nk] = lax.dot_general(qa[rows], ka_next[:nk], _NT, preferred_element_type=F32)


def _near_patches(s_ref, bias_diag, bias_prev, tq, diagonal, c0):
    t = BIAS_TILE
    n_sub = tq // t
    hh = c0 // tq
    cols = lambda a: slice(a * t, (a + 1) * t)
    for a in range((c0 % tq) // t, (c0 % tq + ROW_CHUNK) // t):
        rows = slice(hh * tq + a * t, hh * tq + (a + 1) * t)
        if not diagonal:
            if a == 0:
                s_ref[rows, cols(n_sub - 1)] = s_ref[rows, cols(n_sub - 1)] + bias_prev(hh)
            continue
        s_ref[rows, cols(a)] = s_ref[rows, cols(a)] + bias_diag(hh)
        if a >= 1:
            s_ref[rows, cols(a - 1)] = s_ref[rows, cols(a - 1)] + bias_prev(hh)
        hi = _diag_keys(tq, c0)
        if (a + 1) * t < hi:
            s_ref[rows, (a + 1) * t:hi] = jnp.full((t, hi - (a + 1) * t), MASK_VALUE, F32)


def _run_causal_tiles(i, tq, qa, keys_of, values_of, s_ref, m_sc, acc_sc, bias_diag, bias_prev):
    diag_keys = functools.partial(_diag_keys, tq)
    _flash_tile(qa, keys_of(0), None, s_ref, m_sc, acc_sc)

    def far_tile(j, carry):
        _flash_tile(qa, keys_of(j + 1), values_of(j), s_ref, m_sc, acc_sc)
        return carry

    lax.fori_loop(0, jnp.maximum(i - 1, 0), far_tile, 0)

    @pl.when(i >= 1)
    def _():
        _flash_tile(qa, keys_of(i), values_of(i - 1), s_ref, m_sc, acc_sc,
                    patch=functools.partial(_near_patches, s_ref, bias_diag, bias_prev, tq, False),
                    keys_next=diag_keys)

    _flash_tile(qa, None, values_of(i), s_ref, m_sc, acc_sc,
                patch=functools.partial(_near_patches, s_ref, bias_diag, bias_prev, tq, True),
                keys=diag_keys)


def _attn_scratch(tq):
    return [pltpu.VMEM((2 * tq, tq), F32), pltpu.VMEM((2 * tq, 1), F32), pltpu.VMEM((2 * tq, 2 * LANES), F32)]


def _diff_attn_kernel(lam_ref, g_ref, q_ref, k_ref, v_ref, bd_ref, bp_ref, o_ref,
                      s_ref, m_sc, acc_sc, *, tq, lam_init):
    i = pl.program_id(2)
    q = q_ref[0]
    lane = lax.broadcasted_iota(jnp.int32, (1, LANES), 1)
    zero = jnp.zeros_like(q)
    qq = jnp.concatenate([jnp.where(lane < HEAD_DIM, q, zero),
                          jnp.where(lane >= HEAD_DIM, q, zero)], axis=0)
    m_sc[...] = jnp.full(m_sc.shape, MASK_VALUE, F32)
    acc_sc[...] = jnp.zeros(acc_sc.shape, F32)

    tile = lambda ref, j: ref[0, pl.ds(pl.multiple_of(j * tq, tq), tq), :]
    _run_causal_tiles(i, tq, qq, functools.partial(tile, k_ref), functools.partial(tile, v_ref),
                      s_ref, m_sc, acc_sc, lambda hh: bd_ref[0], lambda hh: bp_ref[0])

    acc = acc_sc[...]
    o = acc[:, :LANES] / acc[:, LANES:LANES + 1]
    lam = (jnp.exp(jnp.sum(lam_ref[0:1, :] * lam_ref[1:2, :], axis=-1, keepdims=True))
           - jnp.exp(jnp.sum(lam_ref[2:3, :] * lam_ref[3:4, :], axis=-1, keepdims=True)) + lam_init)
    od = o[:tq] - lam * o[tq:]
    ms = jnp.mean(jnp.square(od), -1, keepdims=True)
    o_ref[0] = ((od * lax.rsqrt(ms + NORM_EPS) * g_ref[...]) * (1.0 - lam_init)).astype(BF16)


def _diff_attention(qkv, lams, norm_g, bias_diag, bias_prev, lam_init):
    b, s, _ = qkv.shape
    tq = min(ATTN_SUPER, s)
    t = BIAS_TILE
    nh = DIFF_HEADS
    return pl.pallas_call(
        functools.partial(_diff_attn_kernel, tq=tq, lam_init=lam_init),
        grid=(b, nh, s // tq),
        in_specs=[
            pl.BlockSpec(lams.shape, lambda bi, h, i: (0, 0)),
            pl.BlockSpec(norm_g.shape, lambda bi, h, i: (0, 0)),
            pl.BlockSpec((1, tq, LANES), lambda bi, h, i: (bi, i, h)),
            pl.BlockSpec((1, s, LANES), lambda bi, h, i: (bi, 0, nh + h)),
            pl.BlockSpec((1, s, LANES), lambda bi, h, i: (bi, 0, 2 * nh + h)),
            pl.BlockSpec((1, t, t), lambda bi, h, i: (h, 0, 0)),
            pl.BlockSpec((1, t, t), lambda bi, h, i: (h, 0, 0)),
        ],
        out_specs=pl.BlockSpec((1, tq, LANES), lambda bi, h, i: (bi, i, h)),
        out_shape=jax.ShapeDtypeStruct((b, s, BRANCH_WIDTH), BF16),
        scratch_shapes=_attn_scratch(tq),
        compiler_params=_cparams(("parallel", "parallel", "arbitrary")),
        name="diff_attention",
    )(lams, norm_g, qkv, qkv, qkv, bias_diag, bias_prev)


def _kmean_kernel(k_ref, o_ref):
    o_ref[0, 0] = jnp.mean(k_ref[0].astype(F32), axis=0, keepdims=True)


def _block_means(qkv):
    b, s, _ = qkv.shape
    nb = s // MOBA_BLOCK
    w = BRANCH_WIDTH
    return pl.pallas_call(
        _kmean_kernel,
        grid=(b, nb),
        in_specs=[pl.BlockSpec((1, MOBA_BLOCK, w), lambda bi, j: (bi, j, 1))],
        out_specs=pl.BlockSpec((1, 1, 1, w), lambda bi, j: (bi, j, 0, 0)),
        out_shape=jax.ShapeDtypeStruct((b, nb, 1, w), F32),
        compiler_params=_cparams(("parallel", "parallel")),
        name="moba_kmean",
    )(qkv)


def _moba_kernel(q_ref, km_ref, k_ref, v_ref, bd_ref, bp_ref, o_ref,
                 qa_sc, s_ref, m_sc, acc_sc, *, tq):
    i = pl.program_id(2)
    shift = int(math.log2(MOBA_BLOCK))
    q = q_ref[0]
    lane = lax.broadcasted_iota(jnp.int32, (1, LANES), 1)
    blk = lax.broadcasted_iota(jnp.int32, (tq, LANES), 1).astype(F32)
    row = lax.broadcasted_iota(jnp.int32, (tq, 1), 0)
    own = lax.shift_right_logical(i * tq + row, shift).astype(F32)
    zero = jnp.zeros_like(q)
    km = km_ref[0].astype(BF16)
    ninf = jnp.float32(-jnp.inf)
    for hh in range(2):
        qh = jnp.where((lane >= hh * HEAD_DIM) & (lane < (hh + 1) * HEAD_DIM), q, zero)
        gate = lax.dot_general(qh, km, _NT, preferred_element_type=F32)
        g = jnp.where(blk < own, gate, ninf)
        sel = blk >= own
        for _ in range(MOBA_TOPK):
            mx = jnp.max(g, axis=-1, keepdims=True)
            first = jnp.min(jnp.where(g == mx, blk, float(LANES)), axis=-1, keepdims=True)
            hit = blk == first
            sel = sel | (hit & (mx > ninf))
            g = jnp.where(hit, ninf, g)
        qa_sc[hh * tq:(hh + 1) * tq, :] = jnp.concatenate(
            [qh, jnp.where(sel, 0.0, MASK_VALUE).astype(BF16)], axis=1)
    m_sc[...] = jnp.full(m_sc.shape, MASK_VALUE, F32)
    acc_sc[...] = jnp.zeros(acc_sc.shape, F32)

    def keys_of(j):
        start = pl.multiple_of(j * tq, tq)
        kblk = lax.shift_right_logical(start + row, shift)
        return jnp.concatenate([k_ref[0, pl.ds(start, tq), :],
                                jnp.where(lane == kblk, 1.0, 0.0).astype(BF16)], axis=1)

    values_of = lambda j: v_ref[0, pl.ds(pl.multiple_of(j * tq, tq), tq), :]
    _run_causal_tiles(i, tq, qa_sc, keys_of, values_of, s_ref, m_sc, acc_sc,
                      lambda hh: bd_ref[0, hh], lambda hh: bp_ref[0, hh])

    acc = acc_sc[...]
    o = acc[:, :LANES] / acc[:, LANES:LANES + 1]
    o_ref[0] = jnp.where(lane < HEAD_DIM, o[:tq], o[tq:]).astype(BF16)


def _moba_attention(qkv, kmean, bias_diag, bias_prev):
    b, s, _ = qkv.shape
    tq = min(ATTN_SUPER, s)
    t = BIAS_TILE
    npair = MOBA_HEADS // 2
    return pl.pallas_call(
        functools.partial(_moba_kernel, tq=tq),
        grid=(b, npair, s // tq),
        in_specs=[
            pl.BlockSpec((1, tq, LANES), lambda bi, p, i: (bi, i, p)),
            pl.BlockSpec((1, LANES, LANES), lambda bi, p, i: (bi, 0, p)),
            pl.BlockSpec((1, s, LANES), lambda bi, p, i: (bi, 0, npair + p)),
            pl.BlockSpec((1, s, LANES), lambda bi, p, i: (bi, 0, 2 * npair + p)),
            pl.BlockSpec((1, 2, t, t), lambda bi, p, i: (p, 0, 0, 0)),
            pl.BlockSpec((1, 2, t, t), lambda bi, p, i: (p, 0, 0, 0)),
        ],
        out_specs=pl.BlockSpec((1, tq, LANES), lambda bi, p, i: (bi, i, p)),
        out_shape=jax.ShapeDtypeStruct((b, s, BRANCH_WIDTH), BF16),
        scratch_shapes=[pltpu.VMEM((2 * tq, 2 * LANES), BF16)] + _attn_scratch(tq),
        compiler_params=_cparams(("parallel", "parallel", "arbitrary")),
        name="moba_attention",
    )(qkv, kmean, qkv, qkv, bias_diag, bias_prev)


def _merge_kernel(h_ref, oa_ref, ob_ref, oc_ref, od_ref, wg_ref, bg_ref, wb_ref, out_ref):
    h = h_ref[...]
    acc = None
    for br, o_ref in enumerate((oa_ref, ob_ref, oc_ref, od_ref)):
        gate = _sigmoid(jnp.dot(h, wg_ref[br], preferred_element_type=F32) + bg_ref[br])
        term = gate * jnp.dot(o_ref[...], wb_ref[br], preferred_element_type=F32)
        acc = term if acc is None else acc + term
    out_ref[...] = acc.astype(BF16)


def _merge(h, branches, w_gate, b_gate, w_branch):
    t, d = h.shape
    w = BRANCH_WIDTH
    tm = min(512, t)
    tn = 512
    nbr = len(branches)
    row = lambda j, i: (i, 0)
    return pl.pallas_call(
        _merge_kernel,
        grid=(d // tn, t // tm),
        in_specs=[pl.BlockSpec((tm, d), row)] + [pl.BlockSpec((tm, w), row)] * nbr + [
            pl.BlockSpec((nbr, d, tn), lambda j, i: (0, 0, j)),
            pl.BlockSpec((nbr, 1, tn), lambda j, i: (0, 0, j)),
            pl.BlockSpec((nbr, w, tn), lambda j, i: (0, 0, j)),
        ],
        out_specs=pl.BlockSpec((tm, tn), lambda j, i: (i, j)),
        out_shape=jax.ShapeDtypeStruct((t, d), BF16),
        compiler_params=_cparams(("parallel", "parallel")),
        name="gated_merge",
    )(h, *branches, w_gate, b_gate, w_branch)


def _route(aff, sel):
    ng, pg = N_EXPERT_GROUPS, EXPERTS_PER_GROUP
    srow = [sel[e:e + 1] for e in range(N_EXPERTS)]
    arow = [aff[e:e + 1] for e in range(N_EXPERTS)]
    best_score, best_grp = None, None
    for g in range(ng):
        rows = srow[g * pg:(g + 1) * pg]
        score = None
        for a in range(pg):
            for b in range(a + 1, pg):
                pair = rows[a] + rows[b]
                score = pair if score is None else jnp.maximum(score, pair)
        if best_score is None:
            best_score, best_grp = score, jnp.zeros_like(score, dtype=jnp.int32)
        else:
            better = score > best_score
            best_score = jnp.where(better, score, best_score)
            best_grp = jnp.where(better, g, best_grp)

    def pick(rows_by_group):
        out = rows_by_group[0]
        for g in range(1, ng):
            out = jnp.where(best_grp == g, rows_by_group[g], out)
        return out

    cand_s = [pick([srow[g * pg + k] for g in range(ng)]) for k in range(pg)]
    cand_a = [pick([arow[g * pg + k] for g in range(ng)]) for k in range(pg)]

    def argmax_first(vals, exclude):
        bv, bi, ba = None, None, None
        for k in range(pg):
            v = vals[k] if exclude is None else jnp.where(exclude == k, -jnp.inf, vals[k])
            if bv is None:
                bv, bi, ba = v, jnp.zeros_like(best_grp), cand_a[k]
            else:
                better = v > bv
                bv = jnp.where(better, v, bv)
                bi = jnp.where(better, k, bi)
                ba = jnp.where(better, cand_a[k], ba)
        return bi, ba

    i0, a0 = argmax_first(cand_s, None)
    i1, a1 = argmax_first(cand_s, i0)
    tot = a0 + a1
    return best_grp * pg + i0, best_grp * pg + i1, a0 / tot, a1 / tot


def _outproj_kernel(m_ref, x_ref, wo_ref, g1_ref, lg_ref, lb_ref, sc2_ref, sh2_ref, wr_ref, br_ref,
                    x1_ref, h2_ref, ri_ref, rw_ref, *, tm):
    y = jnp.dot(m_ref[0], wo_ref[...], preferred_element_type=F32)
    x1 = _layer_norm(ALPHA * x_ref[0] + (1.0 + g1_ref[0]) * y, lg_ref[...], lb_ref[...])
    x1_ref[0] = x1
    h2 = x1 * (1.0 + sc2_ref[0]) + sh2_ref[0]
    h2_ref[0] = h2
    logits = lax.dot_general(wr_ref[...], h2.astype(BF16), _NT, preferred_element_type=F32)
    aff = _sigmoid(logits)
    e0, e1, w0, w1 = _route(aff, aff + br_ref[...])
    ri_ref[0] = jnp.concatenate([e0, e1, jnp.zeros((6, tm), jnp.int32)], axis=0)
    rw_ref[0] = jnp.concatenate([w0, w1, jnp.zeros((6, tm), F32)], axis=0)


def _outproj(merged, x, w_out, g1, ln_g, ln_b, sc2, sh2, w_router_t, b_router):
    b, s, d = x.shape
    tm = min(512, s)
    row = lambda bi, i: (bi, i, 0)
    vec = lambda bi, i: (bi, 0, 0)
    full = lambda a: pl.BlockSpec(a.shape, lambda bi, i: (0,) * a.ndim)
    lg, lb = ln_g.reshape(1, d), ln_b.reshape(1, d)
    br = b_router.reshape(N_EXPERTS, 1)
    return pl.pallas_call(
        functools.partial(_outproj_kernel, tm=tm),
        grid=(b, s // tm),
        in_specs=[
            pl.BlockSpec((1, tm, d), row), pl.BlockSpec((1, tm, d), row), full(w_out),
            pl.BlockSpec((1, 1, d), vec), full(lg), full(lb),
            pl.BlockSpec((1, 1, d), vec), pl.BlockSpec((1, 1, d), vec),
            full(w_router_t), full(br),
        ],
        out_specs=[
            pl.BlockSpec((1, tm, d), row), pl.BlockSpec((1, tm, d), row),
            pl.BlockSpec((1, 8, tm), lambda bi, i: (bi, 0, i)),
            pl.BlockSpec((1, 8, tm), lambda bi, i: (bi, 0, i)),
        ],
        out_shape=[
            jax.ShapeDtypeStruct((b, s, d), F32), jax.ShapeDtypeStruct((b, s, d), F32),
            jax.ShapeDtypeStruct((b, 8, s), jnp.int32), jax.ShapeDtypeStruct((b, 8, s), F32),
        ],
        compiler_params=_cparams(("parallel", "parallel")),
        name="outproj_ln_router",
    )(merged.reshape(b, s, d), x, w_out, g1, lg, lb, sc2, sh2, w_router_t, br)


def _row_copy(src, src_row, dst, dst_row, sem):
    return pltpu.make_async_copy(src.at[pl.ds(src_row, 1)], dst.at[pl.ds(dst_row, 1)], sem)


def _dispatch_kernel(tail_ref, nu_ref, dest_ref, h2_ref, xs_ref, zeros_sc, sem, zsem, *, rows):
    @pl.when(pl.program_id(0) == 0)
    def _():
        zeros_sc[...] = jnp.zeros(zeros_sc.shape, F32)
        fill = lambda row0: pltpu.make_async_copy(
            zeros_sc, xs_ref.at[pl.ds(pl.multiple_of(row0, SLOT_BLOCK), SLOT_BLOCK)], zsem)
        unused = (nu_ref[0], xs_ref.shape[0] // SLOT_BLOCK)
        for e in range(N_EXPERTS):
            pl.when(tail_ref[e] >= 0)(lambda e=e: fill(tail_ref[e]).start())
        lax.fori_loop(*unused, lambda b, c: (fill(b * SLOT_BLOCK).start(), c)[1], 0)
        for e in range(N_EXPERTS):
            pl.when(tail_ref[e] >= 0)(lambda e=e: fill(tail_ref[e]).wait())
        lax.fori_loop(*unused, lambda b, c: (fill(b * SLOT_BLOCK).wait(), c)[1], 0)

    def issue(r, carry):
        for k in range(TOP_K):
            _row_copy(h2_ref, r, xs_ref, dest_ref[0, 0, TOP_K * r + k], sem).start()
        return carry

    lax.fori_loop(0, rows, issue, 0, unroll=4)
    for k in range(TOP_K):
        pltpu.make_async_copy(h2_ref, xs_ref.at[pl.ds(0, rows)], sem).wait()


def _dispatch(dest, tail_start, n_used, h2, n_slots):
    t, d = h2.shape
    rows = min(512, t)
    return pl.pallas_call(
        functools.partial(_dispatch_kernel, rows=rows),
        grid_spec=pltpu.PrefetchScalarGridSpec(
            num_scalar_prefetch=2,
            grid=(t // rows,),
            in_specs=[
                pl.BlockSpec((1, 1, TOP_K * rows), lambda i, tail, nu: (i, 0, 0), memory_space=pltpu.SMEM),
                pl.BlockSpec((rows, d), lambda i, tail, nu: (i, 0)),
            ],
            out_specs=pl.BlockSpec(memory_space=pl.ANY),
            scratch_shapes=[pltpu.VMEM((SLOT_BLOCK, d), F32), pltpu.SemaphoreType.DMA(()),
                            pltpu.SemaphoreType.DMA(())],
        ),
        out_shape=jax.ShapeDtypeStruct((n_slots, d), F32),
        compiler_params=_cparams(("arbitrary",)),
        name="moe_dispatch",
    )(tail_start, n_used, dest.reshape(t // rows, 1, TOP_K * rows), h2)


def _expert_kernel(be_ref, nu_ref, xs_ref, wg_ref, wu_ref, wd_ref, ys_ref):
    del be_ref
    used = pl.program_id(0) < nu_ref[0]

    @pl.when(used)
    def _():
        xb = xs_ref[...].astype(BF16)
        g = jnp.dot(xb, wg_ref[0], preferred_element_type=F32)
        u = jnp.dot(xb, wu_ref[0], preferred_element_type=F32)
        act = ((g * _sigmoid(g)) * u).astype(BF16)
        ys_ref[...] = jnp.dot(act, wd_ref[0], preferred_element_type=F32)

    @pl.when(jnp.logical_not(used))
    def _():
        ys_ref[...] = jnp.zeros(ys_ref.shape, F32)


def _expert_ffn(blk_expert, n_used, xs, w_g, w_u, w_d):
    n_slots, d = xs.shape
    f = w_g.shape[-1]
    blk = SLOT_BLOCK
    last = lambda b, nu: jnp.minimum(b, nu[0] - 1)
    wspec = lambda shape: pl.BlockSpec(shape, lambda b, be, nu: (be[last(b, nu)], 0, 0),
                                       pipeline_mode=pl.Buffered(1))
    return pl.pallas_call(
        _expert_kernel,
        grid_spec=pltpu.PrefetchScalarGridSpec(
            num_scalar_prefetch=2,
            grid=(n_slots // blk,),
            in_specs=[
                pl.BlockSpec((blk, d), lambda b, be, nu: (last(b, nu), 0)),
                wspec((1, d, f)), wspec((1, d, f)), wspec((1, f, d)),
            ],
            out_specs=pl.BlockSpec((blk, d), lambda b, be, nu: (b, 0)),
        ),
        out_shape=jax.ShapeDtypeStruct((n_slots, d), F32),
        compiler_params=_cparams(("arbitrary",)),
        name="moe_experts",
    )(blk_expert, n_used, xs, w_g, w_u, w_d)


def _combine_kernel(dest_ref, w_ref, x1_ref, g2_ref, lg_ref, lb_ref, ys_ref, o_ref, buf, sem, *, rows):
    def issue(r, carry):
        for k in range(TOP_K):
            _row_copy(ys_ref, dest_ref[0, 0, TOP_K * r + k], buf.at[k], r, sem).start()
        return carry

    lax.fori_loop(0, rows, issue, 0, unroll=4)
    for k in range(TOP_K):
        pltpu.make_async_copy(ys_ref.at[pl.ds(0, rows)], buf.at[k], sem).wait()
    y = buf[0] * w_ref[:, 0:1] + buf[1] * w_ref[:, 1:2]
    o_ref[0] = _layer_norm(ALPHA * x1_ref[0] + (1.0 + g2_ref[0]) * y, lg_ref[...], lb_ref[...])


def _combine(dest, wts, x1, g2, ln_g, ln_b, ys):
    b, s, d = x1.shape
    rows = min(256, s)
    nt = s // rows
    lg, lb = ln_g.reshape(1, d), ln_b.reshape(1, d)
    full = lambda a: pl.BlockSpec(a.shape, lambda bi, i: (0,) * a.ndim)
    return pl.pallas_call(
        functools.partial(_combine_kernel, rows=rows),
        grid=(b, nt),
        in_specs=[
            pl.BlockSpec((1, 1, TOP_K * rows), lambda bi, i: (bi * nt + i, 0, 0), memory_space=pltpu.SMEM),
            pl.BlockSpec((rows, TOP_K), lambda bi, i: (bi * nt + i, 0)),
            pl.BlockSpec((1, rows, d), lambda bi, i: (bi, i, 0)),
            pl.BlockSpec((1, 1, d), lambda bi, i: (bi, 0, 0)),
            full(lg), full(lb),
            pl.BlockSpec(memory_space=pl.ANY),
        ],
        out_specs=pl.BlockSpec((1, rows, d), lambda bi, i: (bi, i, 0)),
        out_shape=jax.ShapeDtypeStruct((b, s, d), F32),
        scratch_shapes=[pltpu.VMEM((TOP_K, rows, d), F32), pltpu.SemaphoreType.DMA(())],
        compiler_params=_cparams(("arbitrary", "arbitrary")),
        name="moe_combine_ln",
    )(dest.reshape(b * nt, 1, TOP_K * rows), wts, x1, g2, lg, lb, ys)


def _slot_plan(e_flat, n_slots):
    blk = SLOT_BLOCK
    onehot = (e_flat[:, None] == jnp.arange(N_EXPERTS, dtype=jnp.int32)[None, :]).astype(jnp.int32)
    csum = jnp.cumsum(onehot, axis=0)
    rank = jnp.sum(csum * onehot, axis=1) - 1
    counts = csum[-1]
    padded = (counts + blk - 1) // blk * blk
    pend = jnp.cumsum(padded)
    dest = (pend - padded)[e_flat] + rank
    nblk = n_slots // blk
    blk_expert = jnp.minimum(
        jnp.searchsorted(pend, jnp.arange(nblk, dtype=jnp.int32) * blk, side='right'), N_EXPERTS - 1)
    tail_start = jnp.where(padded > 0, pend - blk, -1)
    return (dest.astype(jnp.int32), blk_expert.astype(jnp.int32), (pend[-1:] // blk).astype(jnp.int32),
            tail_start.astype(jnp.int32))


def _t5_bucket(dist):
    max_exact = REL_BUCKETS // 2
    d = jnp.maximum(dist, 1).astype(F32)
    large = max_exact + (jnp.log(d / max_exact) / math.log(REL_MAX_DIST / max_exact)
                         * (REL_BUCKETS - max_exact)).astype(jnp.int32)
    large = jnp.minimum(large, REL_BUCKETS - 1)
    return jnp.where(dist < max_exact, dist, large)


def _bias_tiles(rel_table, t):
    r = jnp.arange(t, dtype=jnp.int32)[:, None]
    c = jnp.arange(t, dtype=jnp.int32)[None, :]
    dd = r - c

    def lookup(dist):
        onehot = (_t5_bucket(dist)[:, :, None] == jnp.arange(REL_BUCKETS, dtype=jnp.int32)).astype(F32)
        return jnp.einsum('rck,kh->hrc', onehot, rel_table, precision=lax.Precision.HIGHEST)

    diag = jnp.where(dd >= 0, lookup(jnp.maximum(dd, 0)), MASK_VALUE)
    return diag, lookup(dd + t)


def kernel(x, c, rel_bias, w_router, b_router, w_ada, b_ada, w_in, lam_q1, lam_k1, lam_q2, lam_k2,
           diff_norm_g, w_pool, pool_scale, sgu_ln_g, sgu_ln_b, w_spatial, b_spatial, w_branch,
           w_gate, b_gate, w_out, ln1_g, ln1_b, w_e_gate, w_e_up, w_e_down, ln2_g, ln2_b):
    b, s, d = x.shape
    t = b * s
    assert d == D_MODEL and s % MOBA_BLOCK == 0 and s // MOBA_BLOCK <= LANES
    assert int(16 * 8 ** (15 / 16)) + 1 < MOBA_BLOCK and REL_BUCKETS == 32 and REL_MAX_DIST == 128

    assert s % min(ATTN_SUPER, s) == 0 and BIAS_TILE == MOBA_BLOCK
    diag, prev = _bias_tiles(rel_bias - rel_bias[REL_BUCKETS - 1][None, :], BIAS_TILE)
    a_diag, a_prev = diag[:DIFF_HEADS], prev[:DIFF_HEADS]
    pair = lambda a: a[DIFF_HEADS:].reshape(MOBA_HEADS // 2, 2, BIAS_TILE, BIAS_TILE)
    d_diag, d_prev = pair(diag), pair(prev)

    mod = _ada_mod(c, w_ada, b_ada)
    w_router_t = w_router.T.astype(BF16)
    n_slots = (-(-t * TOP_K // SLOT_BLOCK) + N_EXPERTS) * SLOT_BLOCK

    for layer in range(DEPTH):
        sh1, sc1, g1, sh2, sc2, g2 = (m[:, None, :] for m in jnp.split(mod[layer], 6, axis=-1))
        lam_init = 0.8 - 0.6 * math.exp(-0.3 * layer)
        lams = jnp.stack([lam_q1[layer], lam_k1[layer], lam_q2[layer], lam_k2[layer]])

        h, qkv_a, zpuv, qkv_d = _inproj(x, sc1, sh1, w_in[layer].astype(BF16))
        o_b, o_c = _branches(zpuv, w_pool[layer], pool_scale[layer], sgu_ln_g[layer], sgu_ln_b[layer],
                             w_spatial[layer], b_spatial[layer])
        o_a = _diff_attention(qkv_a, lams, diff_norm_g[layer].reshape(1, 2 * HEAD_DIM),
                              a_diag, a_prev, lam_init)
        kmean = _block_means(qkv_d).reshape(b, s // MOBA_BLOCK, BRANCH_WIDTH)
        kmean = jnp.pad(kmean, ((0, 0), (0, LANES - s // MOBA_BLOCK), (0, 0)))
        o_d = _moba_attention(qkv_d, kmean, d_diag, d_prev)

        flat = lambda a: a.reshape(t, a.shape[-1])
        merged = _merge(flat(h), [flat(o_a), flat(o_b), flat(o_c), flat(o_d)],
                        w_gate[layer].astype(BF16), b_gate[layer][:, None, :], w_branch[layer].astype(BF16))
        x1, h2, ridx, rwts = _outproj(merged, x, w_out[layer].astype(BF16), g1, ln1_g[layer], ln1_b[layer],
                                      sc2, sh2, w_router_t, b_router)

        e_tk = jnp.transpose(ridx[:, :TOP_K, :], (0, 2, 1)).reshape(t * TOP_K)
        wts = jnp.transpose(rwts[:, :TOP_K, :], (0, 2, 1)).reshape(t, TOP_K)
        dest, blk_expert, n_used, tail_start = _slot_plan(e_tk, n_slots)
        xs = _dispatch(dest, tail_start, n_used, h2.reshape(t, d), n_slots)
        ys = _expert_ffn(blk_expert, n_used, xs, w_e_gate[layer].astype(BF16),
                         w_e_up[layer].astype(BF16), w_e_down[layer].astype(BF16))
        x = _combine(dest, wts, x1, g2, ln2_g[layer], ln2_b[layer], ys)
    return x
```

```python
import functools
import math

import jax
import jax.numpy as jnp
from jax import lax
from jax.experimental import pallas as pl
from jax.experimental.pallas import tpu as pltpu

F32 = jnp.float32
BF16 = jnp.bfloat16

D_MODEL = 2048
DEPTH = 2
BRANCH_WIDTH = 512
HEAD_DIM = 64
N_SECTIONS = 9
DIFF_HEADS = 4
POOL_WINDOWS = (2, 4, 8, 16)
POOL_MAX_WINDOW = 16
SGU_CHUNK = 128
SGU_GROUPS = 4
MOBA_HEADS = 8
MOBA_BLOCK = 256
MOBA_TOPK = 3
REL_BUCKETS = 32
REL_MAX_DIST = 128
N_EXPERTS = 16
N_EXPERT_GROUPS = 4
EXPERTS_PER_GROUP = 4
TOP_K = 2
EXPERT_FF = 1408
ALPHA = (2 * DEPTH) ** 0.25
NORM_EPS = 1e-5

LANES = 128
MASK_VALUE = -1e30
ATTN_SUPER = 1024
BIAS_TILE = 256
ROW_CHUNK = 512
SLOT_BLOCK = 512
VMEM_LIMIT = 56 * 2 ** 20


def _cparams(semantics, vmem=VMEM_LIMIT):
    return pltpu.CompilerParams(dimension_semantics=semantics, vmem_limit_bytes=vmem)


def _sigmoid(x):
    return 1.0 / (1.0 + jnp.exp(-x))


def _layer_norm(r, g, b):
    mu = jnp.mean(r, -1, keepdims=True)
    var = jnp.mean(jnp.square(r - mu), -1, keepdims=True)
    return (r - mu) * lax.rsqrt(var + NORM_EPS) * g + b


def _ada_kernel(ct_ref, w_ref, b_ref, o_ref, *, batch):
    ct = ct_ref[...]
    ca = ct * _sigmoid(ct)
    w = w_ref[0]
    rows = [jnp.sum(w * ca[:, b:b + 1], axis=0, keepdims=True) for b in range(batch)]
    o_ref[0] = jnp.concatenate(rows, axis=0) + b_ref[0]


def _ada_mod(c, w_ada, b_ada):
    batch, d = c.shape
    depth, _, n = w_ada.shape
    tn = 512
    return pl.pallas_call(
        functools.partial(_ada_kernel, batch=batch),
        grid=(depth, n // tn),
        in_specs=[
            pl.BlockSpec((d, batch), lambda l, j: (0, 0)),
            pl.BlockSpec((1, d, tn), lambda l, j: (l, 0, j)),
            pl.BlockSpec((1, 1, tn), lambda l, j: (l, 0, j)),
        ],
        out_specs=pl.BlockSpec((1, batch, tn), lambda l, j: (l, 0, j)),
        out_shape=jax.ShapeDtypeStruct((depth, batch, n), F32),
        compiler_params=_cparams(("parallel", "parallel")),
        name="ada_mod",
    )(c.T, w_ada, b_ada.reshape(depth, 1, n))


def _inproj_kernel(x_ref, sc_ref, sh_ref, w_ref, h_ref, a_ref, p_ref, d_ref, km_ref):
    w = BRANCH_WIDTH
    h = (x_ref[0] * (1.0 + sc_ref[0]) + sh_ref[0]).astype(BF16)
    h_ref[0] = h
    for s in range(N_SECTIONS):
        z = jnp.dot(h, w_ref[:, s * w:(s + 1) * w], preferred_element_type=F32)
        if s in (0, 6):
            z = z * (HEAD_DIM ** -0.5)
        if s < 3:
            a_ref[0, :, s * w:(s + 1) * w] = z.astype(BF16)
        elif s < 6:
            p_ref[0, :, (s - 3) * w:(s - 2) * w] = z
        else:
            zb = z.astype(BF16)
            d_ref[0, :, (s - 6) * w:(s - 5) * w] = zb
            if s == 7:
                kf = zb.astype(F32)
                for blk in range(kf.shape[0] // MOBA_BLOCK):
                    km_ref[0, 0, blk:blk + 1, :] = jnp.mean(
                        kf[blk * MOBA_BLOCK:(blk + 1) * MOBA_BLOCK], axis=0, keepdims=True)


def _inproj(x, sc, sh, w_in):
    b, s, d = x.shape
    tm = min(512, s)
    n3 = 3 * BRANCH_WIDTH
    row = lambda bi, i: (bi, i, 0)
    vec = lambda bi, i: (bi, 0, 0)
    return pl.pallas_call(
        _inproj_kernel,
        grid=(b, s // tm),
        in_specs=[
            pl.BlockSpec((1, tm, d), row),
            pl.BlockSpec((1, 1, d), vec),
            pl.BlockSpec((1, 1, d), vec),
            pl.BlockSpec(w_in.shape, lambda bi, i: (0, 0)),
        ],
        out_specs=[
            pl.BlockSpec((1, tm, d), row),
            pl.BlockSpec((1, tm, n3), row),
            pl.BlockSpec((1, tm, n3), row),
            pl.BlockSpec((1, tm, n3), row),
            pl.BlockSpec((1, 1, tm // MOBA_BLOCK, BRANCH_WIDTH), lambda bi, i: (bi, i, 0, 0)),
        ],
        out_shape=[
            jax.ShapeDtypeStruct((b, s, d), BF16),
            jax.ShapeDtypeStruct((b, s, n3), BF16),
            jax.ShapeDtypeStruct((b, s, n3), F32),
            jax.ShapeDtypeStruct((b, s, n3), BF16),
            jax.ShapeDtypeStruct((b, s // tm, tm // MOBA_BLOCK, BRANCH_WIDTH), F32),
        ],
        compiler_params=_cparams(("parallel", "parallel")),
        name="inproj",
    )(x, sc, sh, w_in)


def _branches_kernel(z_ref, halo_ref, wp_ref, ps_ref, lg_ref, lb_ref, wsp_ref, bsp_ref,
                     ob_ref, oc_ref, *, tm):
    i = pl.program_id(1)
    w = BRANCH_WIDTH
    gd = w // len(POOL_WINDOWS)
    hw = POOL_MAX_WINDOW

    zp = z_ref[0, :, 0:w]
    halo = jnp.where(i > 0, halo_ref[0, :, 0:w], 0.0)
    ext = jnp.concatenate([halo, zp], axis=0)
    pos = (i * tm + lax.broadcasted_iota(jnp.int32, (tm, 1), 0) + 1).astype(F32)
    outs = []
    for g, win in enumerate(POOL_WINDOWS):
        e = ext[:, g * gd:(g + 1) * gd]
        shift = 1
        while shift < win:
            e = e + pltpu.roll(e, shift, axis=0)
            shift *= 2
        cnt = jnp.minimum(pos, float(win))
        pooled = e[hw:, :] / cnt - zp[:, g * gd:(g + 1) * gd]
        outs.append(jnp.dot(pooled.astype(BF16), wp_ref[g].astype(BF16), preferred_element_type=F32))
    ob_ref[0] = (jnp.concatenate(outs, axis=1) * ps_ref[...]).astype(BF16)

    u = jax.nn.gelu(z_ref[0, :, w:2 * w], approximate=True)
    v = _layer_norm(jax.nn.gelu(z_ref[0, :, 2 * w:3 * w], approximate=True), lg_ref[...], lb_ref[...])
    r = lax.broadcasted_iota(jnp.int32, (SGU_CHUNK, SGU_CHUNK), 0)
    c = lax.broadcasted_iota(jnp.int32, (SGU_CHUNK, SGU_CHUNK), 1)
    sg = w // SGU_GROUPS
    for g in range(SGU_GROUPS):
        wc = jnp.where(r >= c, wsp_ref[g], 0.0).astype(BF16)
        for ch in range(tm // SGU_CHUNK):
            rows = slice(ch * SGU_CHUNK, (ch + 1) * SGU_CHUNK)
            cols = slice(g * sg, (g + 1) * sg)
            vm = jnp.dot(wc, v[rows, cols].astype(BF16), preferred_element_type=F32) + bsp_ref[g]
            oc_ref[0, rows, cols] = (u[rows, cols] * vm).astype(BF16)


def _branches(zpuv, w_pool, pool_scale, ln_g, ln_b, w_sp, b_sp):
    b, s, n3 = zpuv.shape
    tm = min(512, s)
    w = BRANCH_WIDTH
    hw = POOL_MAX_WINDOW
    bsp = jnp.broadcast_to(b_sp[:, :, None], (SGU_GROUPS, SGU_CHUNK, w // SGU_GROUPS))
    full = lambda a: pl.BlockSpec(a.shape, lambda bi, i: (0,) * a.ndim)
    ps, lg, lb = pool_scale.reshape(1, w), ln_g.reshape(1, w), ln_b.reshape(1, w)
    return pl.pallas_call(
        functools.partial(_branches_kernel, tm=tm),
        grid=(b, s // tm),
        in_specs=[
            pl.BlockSpec((1, tm, n3), lambda bi, i: (bi, i, 0)),
            pl.BlockSpec((1, hw, n3), lambda bi, i: (bi, jnp.maximum(i * (tm // hw) - 1, 0), 0)),
            full(w_pool), full(ps), full(lg), full(lb), full(w_sp), full(bsp),
        ],
        out_specs=[pl.BlockSpec((1, tm, w), lambda bi, i: (bi, i, 0))] * 2,
        out_shape=[jax.ShapeDtypeStruct((b, s, w), BF16)] * 2,
        compiler_params=_cparams(("parallel", "parallel")),
        name="pool_sgu",
    )(zpuv, zpuv, w_pool, ps, lg, lb, w_sp, bsp)


_NT = (((1,), (1,)), ((), ()))


def _diag_keys(tq, c0):
    return c0 % tq + ROW_CHUNK


def _flash_tile(qa, ka_next, vb, s_ref, s_next, m_sc, acc_sc, patch=None, keys=None, keys_next=None):
    n_rows, tk = s_next.shape
    if vb is not None:
        lane = lax.broadcasted_iota(jnp.int32, vb.shape, 1)
        va = jnp.concatenate([vb, jnp.where(lane == 0, 1.0, 0.0).astype(BF16)], axis=1)
    for c0 in range(0, n_rows, ROW_CHUNK):
        rows = slice(c0, c0 + ROW_CHUNK)
        if vb is not None:
            if patch is not None:
                patch(c0)
            nk = tk if keys is None else keys(c0)
            s = s_ref[rows, :nk]
            m_old = m_sc[rows]
            m_new = jnp.maximum(m_old, jnp.max(s, axis=-1, keepdims=True))
            p = jnp.exp(s - m_new).astype(BF16)
            acc_sc[rows] = (jnp.exp(m_old - m_new) * acc_sc[rows]
                            + jnp.dot(p, va[:nk], preferred_element_type=F32))
            m_sc[rows] = m_new
        if ka_next is not None:
            nk = tk if keys_next is None else keys_next(c0)
            s_next[rows, :nk] = lax.dot_general(qa[rows], ka_next[:nk], _NT, preferred_element_type=F32)


def _near_patches(s_ref, bias_diag, bias_prev, tq, diagonal, c0):
    t = BIAS_TILE
    n_sub = tq // t
    hh = c0 // tq
    cols = lambda a: slice(a * t, (a + 1) * t)
    for a in range((c0 % tq) // t, (c0 % tq + ROW_CHUNK) // t):
        rows = slice(hh * tq + a * t, hh * tq + (a + 1) * t)
        if not diagonal:
            if a == 0:
                s_ref[rows, cols(n_sub - 1)] = s_ref[rows, cols(n_sub - 1)] + bias_prev(hh)
            continue
        s_ref[rows, cols(a)] = s_ref[rows, cols(a)] + bias_diag(hh)
        if a >= 1:
            s_ref[rows, cols(a - 1)] = s_ref[rows, cols(a - 1)] + bias_prev(hh)
        hi = _diag_keys(tq, c0)
        if (a + 1) * t < hi:
            s_ref[rows, (a + 1) * t:hi] = jnp.full((t, hi - (a + 1) * t), MASK_VALUE, F32)


def _run_causal_tiles(i, tq, qa, keys_of, values_of, s2_ref, m_sc, acc_sc, bias_diag, bias_prev):
    diag_keys = functools.partial(_diag_keys, tq)
    s0, s1 = s2_ref.at[0], s2_ref.at[1]
    tile = functools.partial(_flash_tile, qa, m_sc=m_sc, acc_sc=acc_sc)
    n_far = jnp.maximum(i - 1, 0)

    tile(keys_of(i), None, None, s0, keys_next=diag_keys)
    tile(keys_of(jnp.maximum(i - 1, 0)), values_of(i), s0, s1, keys=diag_keys,
         patch=functools.partial(_near_patches, s0, bias_diag, bias_prev, tq, True))

    @pl.when(i >= 1)
    def _():
        tile(keys_of(0), values_of(i - 1), s1, s0,
             patch=functools.partial(_near_patches, s1, bias_diag, bias_prev, tq, False))

    def far_pair(k, carry):
        tile(keys_of(2 * k + 1), values_of(2 * k), s0, s1)
        tile(keys_of(2 * k + 2), values_of(2 * k + 1), s1, s0)
        return carry

    n_pairs = jnp.maximum(n_far - 1, 0) // 2
    lax.fori_loop(0, n_pairs, far_pair, 0)
    first_left = 2 * n_pairs

    @pl.when(n_far - first_left == 2)
    def _():
        tile(keys_of(first_left + 1), values_of(first_left), s0, s1)
        tile(None, values_of(first_left + 1), s1, s1)

    @pl.when(n_far - first_left == 1)
    def _():
        tile(None, values_of(first_left), s0, s0)


def _attn_scratch(tq):
    return [pltpu.VMEM((2, 2 * tq, tq), F32), pltpu.VMEM((2 * tq, 1), F32), pltpu.VMEM((2 * tq, 2 * LANES), F32)]


def _diff_attn_kernel(lam_ref, g_ref, q_ref, k_ref, v_ref, bd_ref, bp_ref, o_ref,
                      s_ref, m_sc, acc_sc, *, tq, lam_init):
    i = pl.program_id(2)
    q = q_ref[0]
    lane = lax.broadcasted_iota(jnp.int32, (1, LANES), 1)
    zero = jnp.zeros_like(q)
    qq = jnp.concatenate([jnp.where(lane < HEAD_DIM, q, zero),
                          jnp.where(lane >= HEAD_DIM, q, zero)], axis=0)
    m_sc[...] = jnp.full(m_sc.shape, MASK_VALUE, F32)
    acc_sc[...] = jnp.zeros(acc_sc.shape, F32)

    tile = lambda ref, j: ref[0, pl.ds(pl.multiple_of(j * tq, tq), tq), :]
    _run_causal_tiles(i, tq, qq, functools.partial(tile, k_ref), functools.partial(tile, v_ref),
                      s_ref, m_sc, acc_sc, lambda hh: bd_ref[0], lambda hh: bp_ref[0])

    acc = acc_sc[...]
    o = acc[:, :LANES] / acc[:, LANES:LANES + 1]
    lam = (jnp.exp(jnp.sum(lam_ref[0:1, :] * lam_ref[1:2, :], axis=-1, keepdims=True))
           - jnp.exp(jnp.sum(lam_ref[2:3, :] * lam_ref[3:4, :], axis=-1, keepdims=True)) + lam_init)
    od = o[:tq] - lam * o[tq:]
    ms = jnp.mean(jnp.square(od), -1, keepdims=True)
    o_ref[0] = ((od * lax.rsqrt(ms + NORM_EPS) * g_ref[...]) * (1.0 - lam_init)).astype(BF16)


def _diff_attention(qkv, lams, norm_g, bias_diag, bias_prev, lam_init):
    b, s, _ = qkv.shape
    tq = min(ATTN_SUPER, s)
    t = BIAS_TILE
    nh = DIFF_HEADS
    return pl.pallas_call(
        functools.partial(_diff_attn_kernel, tq=tq, lam_init=lam_init),
        grid=(b, nh, s // tq),
        in_specs=[
            pl.BlockSpec(lams.shape, lambda bi, h, i: (0, 0)),
            pl.BlockSpec(norm_g.shape, lambda bi, h, i: (0, 0)),
            pl.BlockSpec((1, tq, LANES), lambda bi, h, i: (bi, i, h)),
            pl.BlockSpec((1, s, LANES), lambda bi, h, i: (bi, 0, nh + h)),
            pl.BlockSpec((1, s, LANES), lambda bi, h, i: (bi, 0, 2 * nh + h)),
            pl.BlockSpec((1, t, t), lambda bi, h, i: (h, 0, 0)),
            pl.BlockSpec((1, t, t), lambda bi, h, i: (h, 0, 0)),
        ],
        out_specs=pl.BlockSpec((1, tq, LANES), lambda bi, h, i: (bi, i, h)),
        out_shape=jax.ShapeDtypeStruct((b, s, BRANCH_WIDTH), BF16),
        scratch_shapes=_attn_scratch(tq),
        compiler_params=_cparams(("parallel", "parallel", "arbitrary")),
        name="diff_attention",
    )(lams, norm_g, qkv, qkv, qkv, bias_diag, bias_prev)


def _moba_kernel(q_ref, km_ref, k_ref, v_ref, bd_ref, bp_ref, o_ref,
                 qa_sc, s_ref, m_sc, acc_sc, *, tq):
    i = pl.program_id(2)
    shift = int(math.log2(MOBA_BLOCK))
    q = q_ref[0]
    lane = lax.broadcasted_iota(jnp.int32, (1, LANES), 1)
    row = lax.broadcasted_iota(jnp.int32, (tq, 1), 0)
    blk = lax.broadcasted_iota(jnp.int32, (LANES, tq), 0).astype(F32)
    own = lax.shift_right_logical(i * tq + lax.broadcasted_iota(jnp.int32, (1, tq), 1),
                                  shift).astype(F32)
    zero = jnp.zeros_like(q)
    km = km_ref[0].astype(BF16)
    ninf = jnp.float32(-jnp.inf)
    for hh in range(2):
        qh = jnp.where((lane >= hh * HEAD_DIM) & (lane < (hh + 1) * HEAD_DIM), q, zero)
        gate = lax.dot_general(km, qh, _NT, preferred_element_type=F32)
        g = jnp.where(blk < own, gate, ninf)
        sel = blk >= own
        for _ in range(MOBA_TOPK):
            mx = jnp.max(g, axis=0, keepdims=True)
            first = jnp.min(jnp.where(g == mx, blk, float(LANES)), axis=0, keepdims=True)
            hit = blk == first
            sel = sel | (hit & (mx > ninf))
            g = jnp.where(hit, ninf, g)
        qa_sc[hh * tq:(hh + 1) * tq, :] = jnp.concatenate(
            [qh, jnp.where(sel, 0.0, MASK_VALUE).T.astype(BF16)], axis=1)
    m_sc[...] = jnp.full(m_sc.shape, MASK_VALUE, F32)
    acc_sc[...] = jnp.zeros(acc_sc.shape, F32)

    def keys_of(j):
        start = pl.multiple_of(j * tq, tq)
        kblk = lax.shift_right_logical(start + row, shift)
        return jnp.concatenate([k_ref[0, pl.ds(start, tq), :],
                                jnp.where(lane == kblk, 1.0, 0.0).astype(BF16)], axis=1)

    values_of = lambda j: v_ref[0, pl.ds(pl.multiple_of(j * tq, tq), tq), :]
    _run_causal_tiles(i, tq, qa_sc, keys_of, values_of, s_ref, m_sc, acc_sc,
                      lambda hh: bd_ref[0, hh], lambda hh: bp_ref[0, hh])

    acc = acc_sc[...]
    o = acc[:, :LANES] / acc[:, LANES:LANES + 1]
    o_ref[0] = jnp.where(lane < HEAD_DIM, o[:tq], o[tq:]).astype(BF16)


def _moba_attention(qkv, kmean, bias_diag, bias_prev):
    b, s, _ = qkv.shape
    tq = min(ATTN_SUPER, s)
    t = BIAS_TILE
    npair = MOBA_HEADS // 2
    return pl.pallas_call(
        functools.partial(_moba_kernel, tq=tq),
        grid=(b, npair, s // tq),
        in_specs=[
            pl.BlockSpec((1, tq, LANES), lambda bi, p, i: (bi, i, p)),
            pl.BlockSpec((1, LANES, LANES), lambda bi, p, i: (bi, 0, p)),
            pl.BlockSpec((1, s, LANES), lambda bi, p, i: (bi, 0, npair + p)),
            pl.BlockSpec((1, s, LANES), lambda bi, p, i: (bi, 0, 2 * npair + p)),
            pl.BlockSpec((1, 2, t, t), lambda bi, p, i: (p, 0, 0, 0)),
            pl.BlockSpec((1, 2, t, t), lambda bi, p, i: (p, 0, 0, 0)),
        ],
        out_specs=pl.BlockSpec((1, tq, LANES), lambda bi, p, i: (bi, i, p)),
        out_shape=jax.ShapeDtypeStruct((b, s, BRANCH_WIDTH), BF16),
        scratch_shapes=[pltpu.VMEM((2 * tq, 2 * LANES), BF16)] + _attn_scratch(tq),
        compiler_params=_cparams(("parallel", "parallel", "arbitrary")),
        name="moba_attention",
    )(qkv, kmean, qkv, qkv, bias_diag, bias_prev)


def _merge_kernel(h_ref, oa_ref, ob_ref, oc_ref, od_ref, wg_ref, bg_ref, wb_ref, out_ref):
    h = h_ref[...]
    acc = None
    for br, o_ref in enumerate((oa_ref, ob_ref, oc_ref, od_ref)):
        gate = _sigmoid(jnp.dot(h, wg_ref[br], preferred_element_type=F32) + bg_ref[br])
        term = gate * jnp.dot(o_ref[...], wb_ref[br], preferred_element_type=F32)
        acc = term if acc is None else acc + term
    out_ref[...] = acc.astype(BF16)


def _merge(h, branches, w_gate, b_gate, w_branch):
    t, d = h.shape
    w = BRANCH_WIDTH
    tm = min(512, t)
    tn = 512
    nbr = len(branches)
    row = lambda j, i: (i, 0)
    return pl.pallas_call(
        _merge_kernel,
        grid=(d // tn, t // tm),
        in_specs=[pl.BlockSpec((tm, d), row)] + [pl.BlockSpec((tm, w), row)] * nbr + [
            pl.BlockSpec((nbr, d, tn), lambda j, i: (0, 0, j)),
            pl.BlockSpec((nbr, 1, tn), lambda j, i: (0, 0, j)),
            pl.BlockSpec((nbr, w, tn), lambda j, i: (0, 0, j)),
        ],
        out_specs=pl.BlockSpec((tm, tn), lambda j, i: (i, j)),
        out_shape=jax.ShapeDtypeStruct((t, d), BF16),
        compiler_params=_cparams(("parallel", "parallel")),
        name="gated_merge",
    )(h, *branches, w_gate, b_gate, w_branch)


def _route(aff, sel):
    ng, pg = N_EXPERT_GROUPS, EXPERTS_PER_GROUP
    srow = [sel[e:e + 1] for e in range(N_EXPERTS)]
    arow = [aff[e:e + 1] for e in range(N_EXPERTS)]
    best_score, best_grp = None, None
    for g in range(ng):
        rows = srow[g * pg:(g + 1) * pg]
        score = None
        for a in range(pg):
            for b in range(a + 1, pg):
                pair = rows[a] + rows[b]
                score = pair if score is None else jnp.maximum(score, pair)
        if best_score is None:
            best_score, best_grp = score, jnp.zeros_like(score, dtype=jnp.int32)
        else:
            better = score > best_score
            best_score = jnp.where(better, score, best_score)
            best_grp = jnp.where(better, g, best_grp)

    def pick(rows_by_group):
        out = rows_by_group[0]
        for g in range(1, ng):
            out = jnp.where(best_grp == g, rows_by_group[g], out)
        return out

    cand_s = [pick([srow[g * pg + k] for g in range(ng)]) for k in range(pg)]
    cand_a = [pick([arow[g * pg + k] for g in range(ng)]) for k in range(pg)]

    def argmax_first(vals, exclude):
        bv, bi, ba = None, None, None
        for k in range(pg):
            v = vals[k] if exclude is None else jnp.where(exclude == k, -jnp.inf, vals[k])
            if bv is None:
                bv, bi, ba = v, jnp.zeros_like(best_grp), cand_a[k]
            else:
                better = v > bv
                bv = jnp.where(better, v, bv)
                bi = jnp.where(better, k, bi)
                ba = jnp.where(better, cand_a[k], ba)
        return bi, ba

    i0, a0 = argmax_first(cand_s, None)
    i1, a1 = argmax_first(cand_s, i0)
    tot = a0 + a1
    return best_grp * pg + i0, best_grp * pg + i1, a0 / tot, a1 / tot


def _outproj_kernel(m_ref, x_ref, wo_ref, g1_ref, lg_ref, lb_ref, sc2_ref, sh2_ref, wr_ref, br_ref,
                    x1_ref, h2_ref, ri_ref, rw_ref, cnt_ref, *, tm):
    y = jnp.dot(m_ref[0], wo_ref[...], preferred_element_type=F32)
    x1 = _layer_norm(ALPHA * x_ref[0] + (1.0 + g1_ref[0]) * y, lg_ref[...], lb_ref[...])
    x1_ref[0] = x1
    h2 = x1 * (1.0 + sc2_ref[0]) + sh2_ref[0]
    h2_ref[0] = h2
    logits = lax.dot_general(wr_ref[...], h2.astype(BF16), _NT, preferred_element_type=F32)
    aff = _sigmoid(logits)
    e0, e1, w0, w1 = _route(aff, aff + br_ref[...])

    @pl.when((pl.program_id(0) == 0) & (pl.program_id(1) == 0))
    def _():
        cnt_ref[...] = jnp.zeros(cnt_ref.shape, F32)

    erow = lax.broadcasted_iota(jnp.int32, (N_EXPERTS, tm), 0)
    hit0, hit1 = erow == e0, erow == e1
    hits = jnp.where(hit0 | hit1, 1.0, 0.0)
    earlier = (lax.broadcasted_iota(jnp.int32, (tm, tm), 0)
               < lax.broadcasted_iota(jnp.int32, (tm, tm), 1)).astype(BF16)
    before = jnp.dot(hits.astype(BF16), earlier, preferred_element_type=F32) + cnt_ref[:, 0:1]
    rank = lambda hit: jnp.sum(jnp.where(hit, before, 0.0), axis=0, keepdims=True).astype(jnp.int32)
    cnt_ref[...] = cnt_ref[...] + jnp.sum(hits, axis=1, keepdims=True)
    ri_ref[0] = jnp.concatenate([e0, e1, rank(hit0), rank(hit1), jnp.zeros((4, tm), jnp.int32)], axis=0)
    rw_ref[0] = jnp.concatenate([w0, w1, jnp.zeros((6, tm), F32)], axis=0)


def _outproj(merged, x, w_out, g1, ln_g, ln_b, sc2, sh2, w_router_t, b_router):
    b, s, d = x.shape
    tm = min(512, s)
    row = lambda bi, i: (bi, i, 0)
    vec = lambda bi, i: (bi, 0, 0)
    full = lambda a: pl.BlockSpec(a.shape, lambda bi, i: (0,) * a.ndim)
    lg, lb = ln_g.reshape(1, d), ln_b.reshape(1, d)
    br = b_router.reshape(N_EXPERTS, 1)
    return pl.pallas_call(
        functools.partial(_outproj_kernel, tm=tm),
        grid=(b, s // tm),
        in_specs=[
            pl.BlockSpec((1, tm, d), row), pl.BlockSpec((1, tm, d), row), full(w_out),
            pl.BlockSpec((1, 1, d), vec), full(lg), full(lb),
            pl.BlockSpec((1, 1, d), vec), pl.BlockSpec((1, 1, d), vec),
            full(w_router_t), full(br),
        ],
        out_specs=[
            pl.BlockSpec((1, tm, d), row), pl.BlockSpec((1, tm, d), row),
            pl.BlockSpec((1, 8, tm), lambda bi, i: (bi, 0, i)),
            pl.BlockSpec((1, 8, tm), lambda bi, i: (bi, 0, i)),
            pl.BlockSpec((N_EXPERTS, LANES), lambda bi, i: (0, 0)),
        ],
        out_shape=[
            jax.ShapeDtypeStruct((b, s, d), F32), jax.ShapeDtypeStruct((b, s, d), F32),
            jax.ShapeDtypeStruct((b, 8, s), jnp.int32), jax.ShapeDtypeStruct((b, 8, s), F32),
            jax.ShapeDtypeStruct((N_EXPERTS, LANES), F32),
        ],
        compiler_params=_cparams(("arbitrary", "arbitrary")),
        name="outproj_ln_router",
    )(merged.reshape(b, s, d), x, w_out, g1, lg, lb, sc2, sh2, w_router_t, br)


def _row_copy(src, src_row, dst, dst_row, sem):
    return pltpu.make_async_copy(src.at[pl.ds(src_row, 1)], dst.at[pl.ds(dst_row, 1)], sem)


def _dispatch_kernel(tail_ref, nu_ref, dest_ref, h2_ref, xs_ref, zeros_sc, sem, zsem, *, rows):
    @pl.when(pl.program_id(0) == 0)
    def _():
        zeros_sc[...] = jnp.zeros(zeros_sc.shape, F32)
        fill = lambda row0: pltpu.make_async_copy(
            zeros_sc, xs_ref.at[pl.ds(pl.multiple_of(row0, SLOT_BLOCK), SLOT_BLOCK)], zsem)
        unused = (nu_ref[0], xs_ref.shape[0] // SLOT_BLOCK)
        for e in range(N_EXPERTS):
            pl.when(tail_ref[e] >= 0)(lambda e=e: fill(tail_ref[e]).start())
        lax.fori_loop(*unused, lambda b, c: (fill(b * SLOT_BLOCK).start(), c)[1], 0)
        for e in range(N_EXPERTS):
            pl.when(tail_ref[e] >= 0)(lambda e=e: fill(tail_ref[e]).wait())
        lax.fori_loop(*unused, lambda b, c: (fill(b * SLOT_BLOCK).wait(), c)[1], 0)

    def issue(r, carry):
        for k in range(TOP_K):
            _row_copy(h2_ref, r, xs_ref, dest_ref[0, 0, TOP_K * r + k], sem).start()
        return carry

    lax.fori_loop(0, rows, issue, 0, unroll=4)
    for k in range(TOP_K):
        pltpu.make_async_copy(h2_ref, xs_ref.at[pl.ds(0, rows)], sem).wait()


def _dispatch(dest, tail_start, n_used, h2, n_slots):
    t, d = h2.shape
    rows = min(512, t)
    return pl.pallas_call(
        functools.partial(_dispatch_kernel, rows=rows),
        grid_spec=pltpu.PrefetchScalarGridSpec(
            num_scalar_prefetch=2,
            grid=(t // rows,),
            in_specs=[
                pl.BlockSpec((1, 1, TOP_K * rows), lambda i, tail, nu: (i, 0, 0), memory_space=pltpu.SMEM),
                pl.BlockSpec((rows, d), lambda i, tail, nu: (i, 0)),
            ],
            out_specs=pl.BlockSpec(memory_space=pl.ANY),
            scratch_shapes=[pltpu.VMEM((SLOT_BLOCK, d), F32), pltpu.SemaphoreType.DMA(()),
                            pltpu.SemaphoreType.DMA(())],
        ),
        out_shape=jax.ShapeDtypeStruct((n_slots, d), F32),
        compiler_params=_cparams(("arbitrary",)),
        name="moe_dispatch",
    )(tail_start, n_used, dest.reshape(t // rows, 1, TOP_K * rows), h2)


def _cast_kernel(*refs):
    *in_refs, o_ref = refs
    o_ref[0] = jnp.concatenate([r[0, 0].astype(BF16) for r in in_refs], axis=-1)


def _cast_layer(parts, layer, rows):
    _, e, r, _ = parts[0].shape
    c = sum(p.shape[-1] for p in parts)
    return pl.pallas_call(
        _cast_kernel,
        grid=(e, r // rows),
        in_specs=[pl.BlockSpec((1, 1, rows, p.shape[-1]), lambda ei, i: (layer, ei, i, 0)) for p in parts],
        out_specs=pl.BlockSpec((1, rows, c), lambda ei, i: (ei, i, 0)),
        out_shape=jax.ShapeDtypeStruct((e, r, c), BF16),
        compiler_params=_cparams(("parallel", "parallel")),
        name="expert_weight_cast",
    )(*parts)


def _expert_kernel(be_ref, nu_ref, xs_ref, wgu_ref, wd_ref, ys_ref):
    del be_ref
    used = pl.program_id(0) < nu_ref[0]
    f = wd_ref.shape[1]

    @pl.when(used)
    def _():
        gu = jnp.dot(xs_ref[...].astype(BF16), wgu_ref[0], preferred_element_type=F32)
        g = gu[:, :f]
        act = ((g * _sigmoid(g)) * gu[:, f:]).astype(BF16)
        ys_ref[...] = jnp.dot(act, wd_ref[0], preferred_element_type=F32)

    @pl.when(jnp.logical_not(used))
    def _():
        ys_ref[...] = jnp.zeros(ys_ref.shape, F32)


def _expert_ffn(blk_expert, n_used, xs, w_gu, w_d):
    n_slots, d = xs.shape
    f = w_d.shape[1]
    blk = SLOT_BLOCK
    last = lambda b, nu: jnp.minimum(b, nu[0] - 1)
    wspec = lambda shape: pl.BlockSpec(shape, lambda b, be, nu: (be[last(b, nu)], 0, 0),
                                       pipeline_mode=pl.Buffered(1))
    return pl.pallas_call(
        _expert_kernel,
        grid_spec=pltpu.PrefetchScalarGridSpec(
            num_scalar_prefetch=2,
            grid=(n_slots // blk,),
            in_specs=[
                pl.BlockSpec((blk, d), lambda b, be, nu: (last(b, nu), 0)),
                wspec((1, d, 2 * f)), wspec((1, f, d)),
            ],
            out_specs=pl.BlockSpec((blk, d), lambda b, be, nu: (b, 0)),
        ),
        out_shape=jax.ShapeDtypeStruct((n_slots, d), F32),
        compiler_params=_cparams(("arbitrary",)),
        name="moe_experts",
    )(blk_expert, n_used, xs, w_gu, w_d)


def _combine_kernel(dest_ref, dest_next_ref, w_ref, x1_ref, g2_ref, lg_ref, lb_ref, ys_ref, o_ref,
                    buf, sem, *, rows):
    step = pl.program_id(0) * pl.num_programs(1) + pl.program_id(1)
    n_steps = pl.num_programs(0) * pl.num_programs(1)
    slot = step % 2

    def gather(dref, sl):
        def issue(r, carry):
            for k in range(TOP_K):
                _row_copy(ys_ref, dref[0, 0, TOP_K * r + k], buf.at[sl, k], r, sem.at[sl]).start()
            return carry
        lax.fori_loop(0, rows, issue, 0, unroll=4)

    pl.when(step == 0)(lambda: gather(dest_ref, 0))
    pl.when(step + 1 < n_steps)(lambda: gather(dest_next_ref, 1 - slot))
    for k in range(TOP_K):
        pltpu.make_async_copy(ys_ref.at[pl.ds(0, rows)], buf.at[slot, k], sem.at[slot]).wait()
    y = buf[slot, 0] * w_ref[:, 0:1] + buf[slot, 1] * w_ref[:, 1:2]
    o_ref[0] = _layer_norm(ALPHA * x1_ref[0] + (1.0 + g2_ref[0]) * y, lg_ref[...], lb_ref[...])


def _combine(dest, wts, x1, g2, ln_g, ln_b, ys):
    b, s, d = x1.shape
    rows = min(256, s)
    nt = s // rows
    lg, lb = ln_g.reshape(1, d), ln_b.reshape(1, d)
    full = lambda a: pl.BlockSpec(a.shape, lambda bi, i: (0,) * a.ndim)
    dest3 = dest.reshape(b * nt, 1, TOP_K * rows)
    smem = lambda ahead: pl.BlockSpec(
        (1, 1, TOP_K * rows), lambda bi, i: (jnp.minimum(bi * nt + i + ahead, b * nt - 1), 0, 0),
        memory_space=pltpu.SMEM)
    return pl.pallas_call(
        functools.partial(_combine_kernel, rows=rows),
        grid=(b, nt),
        in_specs=[
            smem(0), smem(1),
            pl.BlockSpec((rows, TOP_K), lambda bi, i: (bi * nt + i, 0)),
            pl.BlockSpec((1, rows, d), lambda bi, i: (bi, i, 0)),
            pl.BlockSpec((1, 1, d), lambda bi, i: (bi, 0, 0)),
            full(lg), full(lb),
            pl.BlockSpec(memory_space=pl.ANY),
        ],
        out_specs=pl.BlockSpec((1, rows, d), lambda bi, i: (bi, i, 0)),
        out_shape=jax.ShapeDtypeStruct((b, s, d), F32),
        scratch_shapes=[pltpu.VMEM((2, TOP_K, rows, d), F32), pltpu.SemaphoreType.DMA((2,))],
        compiler_params=_cparams(("arbitrary", "arbitrary")),
        name="moe_combine_ln",
    )(dest3, dest3, wts, x1, g2, lg, lb, ys)


def _slot_plan(e_flat, rank, counts, n_slots):
    blk = SLOT_BLOCK
    padded = (counts + blk - 1) // blk * blk
    pend = jnp.cumsum(padded)
    experts = jnp.arange(N_EXPERTS, dtype=jnp.int32)
    dest = jnp.sum(jnp.where(e_flat[:, None] == experts[None, :], (pend - padded)[None, :], 0), axis=1) + rank
    nblk = n_slots // blk
    blk_expert = jnp.minimum(
        jnp.searchsorted(pend, jnp.arange(nblk, dtype=jnp.int32) * blk, side='right'), N_EXPERTS - 1)
    tail_start = jnp.where(padded > 0, pend - blk, -1)
    return (dest.astype(jnp.int32), blk_expert.astype(jnp.int32), (pend[-1:] // blk).astype(jnp.int32),
            tail_start.astype(jnp.int32))


def _t5_bucket(dist):
    max_exact = REL_BUCKETS // 2
    d = jnp.maximum(dist, 1).astype(F32)
    large = max_exact + (jnp.log(d / max_exact) / math.log(REL_MAX_DIST / max_exact)
                         * (REL_BUCKETS - max_exact)).astype(jnp.int32)
    large = jnp.minimum(large, REL_BUCKETS - 1)
    return jnp.where(dist < max_exact, dist, large)


def _bias_tiles(rel_table, t):
    r = jnp.arange(t, dtype=jnp.int32)[:, None]
    c = jnp.arange(t, dtype=jnp.int32)[None, :]
    dd = r - c

    def lookup(dist):
        onehot = (_t5_bucket(dist)[:, :, None] == jnp.arange(REL_BUCKETS, dtype=jnp.int32)).astype(F32)
        return jnp.einsum('rck,kh->hrc', onehot, rel_table, precision=lax.Precision.HIGHEST)

    diag = jnp.where(dd >= 0, lookup(jnp.maximum(dd, 0)), MASK_VALUE)
    return diag, lookup(dd + t)


def kernel(x, c, rel_bias, w_router, b_router, w_ada, b_ada, w_in, lam_q1, lam_k1, lam_q2, lam_k2,
           diff_norm_g, w_pool, pool_scale, sgu_ln_g, sgu_ln_b, w_spatial, b_spatial, w_branch,
           w_gate, b_gate, w_out, ln1_g, ln1_b, w_e_gate, w_e_up, w_e_down, ln2_g, ln2_b):
    b, s, d = x.shape
    t = b * s
    assert d == D_MODEL and s % MOBA_BLOCK == 0 and s // MOBA_BLOCK <= LANES
    assert int(16 * 8 ** (15 / 16)) + 1 < MOBA_BLOCK and REL_BUCKETS == 32 and REL_MAX_DIST == 128

    assert s % min(ATTN_SUPER, s) == 0 and BIAS_TILE == MOBA_BLOCK
    diag, prev = _bias_tiles(rel_bias - rel_bias[REL_BUCKETS - 1][None, :], BIAS_TILE)
    a_diag, a_prev = diag[:DIFF_HEADS], prev[:DIFF_HEADS]
    pair = lambda a: a[DIFF_HEADS:].reshape(MOBA_HEADS // 2, 2, BIAS_TILE, BIAS_TILE)
    d_diag, d_prev = pair(diag), pair(prev)

    mod = _ada_mod(c, w_ada, b_ada)
    w_router_t = w_router.T.astype(BF16)
    n_slots = (-(-t * TOP_K // SLOT_BLOCK) + N_EXPERTS) * SLOT_BLOCK

    for layer in range(DEPTH):
        sh1, sc1, g1, sh2, sc2, g2 = (m[:, None, :] for m in jnp.split(mod[layer], 6, axis=-1))
        lam_init = 0.8 - 0.6 * math.exp(-0.3 * layer)
        lams = jnp.stack([lam_q1[layer], lam_k1[layer], lam_q2[layer], lam_k2[layer]])

        h, qkv_a, zpuv, qkv_d, kmean = _inproj(x, sc1, sh1, w_in[layer].astype(BF16))
        o_b, o_c = _branches(zpuv, w_pool[layer], pool_scale[layer], sgu_ln_g[layer], sgu_ln_b[layer],
                             w_spatial[layer], b_spatial[layer])
        o_a = _diff_attention(qkv_a, lams, diff_norm_g[layer].reshape(1, 2 * HEAD_DIM),
                              a_diag, a_prev, lam_init)
        kmean = kmean.reshape(b, s // MOBA_BLOCK, BRANCH_WIDTH)
        kmean = jnp.pad(kmean, ((0, 0), (0, LANES - s // MOBA_BLOCK), (0, 0)))
        o_d = _moba_attention(qkv_d, kmean, d_diag, d_prev)

        flat = lambda a: a.reshape(t, a.shape[-1])
        merged = _merge(flat(h), [flat(o_a), flat(o_b), flat(o_c), flat(o_d)],
                        w_gate[layer].astype(BF16), b_gate[layer][:, None, :], w_branch[layer].astype(BF16))
        x1, h2, ridx, rwts, counts = _outproj(
            merged, x, w_out[layer].astype(BF16), g1, ln1_g[layer], ln1_b[layer], sc2, sh2, w_router_t, b_router)

        per_assignment = lambda rows: jnp.transpose(rows, (0, 2, 1)).reshape(t * TOP_K)
        wts = per_assignment(rwts[:, :TOP_K, :]).reshape(t, TOP_K)
        dest, blk_expert, n_used, tail_start = _slot_plan(
            per_assignment(ridx[:, :TOP_K, :]), per_assignment(ridx[:, TOP_K:2 * TOP_K, :]),
            counts[:, 0].astype(jnp.int32), n_slots)
        xs = _dispatch(dest, tail_start, n_used, h2.reshape(t, d), n_slots)
        ys = _expert_ffn(blk_expert, n_used, xs, _cast_layer([w_e_gate, w_e_up], layer, 512),
                         _cast_layer([w_e_down], layer, EXPERT_FF // 4))
        x = _combine(dest, wts, x1, g2, ln2_g[layer], ln2_b[layer], ys)
    return x
```

```python
import functools
import math

import jax
import jax.numpy as jnp
from jax import lax
from jax.experimental import pallas as pl
from jax.experimental.pallas import tpu as pltpu

F32 = jnp.float32
BF16 = jnp.bfloat16

D_MODEL = 2048
DEPTH = 2
BRANCH_WIDTH = 512
HEAD_DIM = 64
N_SECTIONS = 9
DIFF_HEADS = 4
POOL_WINDOWS = (2, 4, 8, 16)
POOL_MAX_WINDOW = 16
SGU_CHUNK = 128
SGU_GROUPS = 4
MOBA_HEADS = 8
MOBA_BLOCK = 256
MOBA_TOPK = 3
REL_BUCKETS = 32
REL_MAX_DIST = 128
N_EXPERTS = 16
N_EXPERT_GROUPS = 4
EXPERTS_PER_GROUP = 4
TOP_K = 2
EXPERT_FF = 1408
ALPHA = (2 * DEPTH) ** 0.25
NORM_EPS = 1e-5

LANES = 128
MASK_VALUE = -1e30
ATTN_SUPER = 1024
BIAS_TILE = 256
ROW_CHUNK = 512
SLOT_BLOCK = 512
VMEM_LIMIT = 56 * 2 ** 20


def _cparams(semantics, vmem=VMEM_LIMIT):
    return pltpu.CompilerParams(dimension_semantics=semantics, vmem_limit_bytes=vmem)


def _sigmoid(x):
    return 1.0 / (1.0 + jnp.exp(-x))


def _layer_norm(r, g, b):
    mu = jnp.mean(r, -1, keepdims=True)
    var = jnp.mean(jnp.square(r - mu), -1, keepdims=True)
    return (r - mu) * lax.rsqrt(var + NORM_EPS) * g + b


def _ada_kernel(ct_ref, w_ref, b_ref, o_ref, *, batch):
    ct = ct_ref[...]
    ca = ct * _sigmoid(ct)
    w = w_ref[0]
    rows = [jnp.sum(w * ca[:, b:b + 1], axis=0, keepdims=True) for b in range(batch)]
    o_ref[0] = jnp.concatenate(rows, axis=0) + b_ref[0]


def _ada_mod(c, w_ada, b_ada):
    batch, d = c.shape
    depth, _, n = w_ada.shape
    tn = 512
    return pl.pallas_call(
        functools.partial(_ada_kernel, batch=batch),
        grid=(depth, n // tn),
        in_specs=[
            pl.BlockSpec((d, batch), lambda l, j: (0, 0)),
            pl.BlockSpec((1, d, tn), lambda l, j: (l, 0, j)),
            pl.BlockSpec((1, 1, tn), lambda l, j: (l, 0, j)),
        ],
        out_specs=pl.BlockSpec((1, batch, tn), lambda l, j: (l, 0, j)),
        out_shape=jax.ShapeDtypeStruct((depth, batch, n), F32),
        compiler_params=_cparams(("parallel", "parallel")),
        name="ada_mod",
    )(c.T, w_ada, b_ada.reshape(depth, 1, n))


def _inproj_kernel(x_ref, sc_ref, sh_ref, w_ref, h_ref, a_ref, p_ref, d_ref, km_ref):
    w = BRANCH_WIDTH
    h = (x_ref[0] * (1.0 + sc_ref[0]) + sh_ref[0]).astype(BF16)
    h_ref[0] = h
    for s in range(N_SECTIONS):
        z = jnp.dot(h, w_ref[:, s * w:(s + 1) * w], preferred_element_type=F32)
        if s in (0, 6):
            z = z * (HEAD_DIM ** -0.5)
        if s < 3:
            a_ref[0, :, s * w:(s + 1) * w] = z.astype(BF16)
        elif s < 6:
            p_ref[0, :, (s - 3) * w:(s - 2) * w] = z
        else:
            zb = z.astype(BF16)
            d_ref[0, :, (s - 6) * w:(s - 5) * w] = zb
            if s == 7:
                kf = zb.astype(F32)
                for blk in range(kf.shape[0] // MOBA_BLOCK):
                    km_ref[0, 0, blk:blk + 1, :] = jnp.mean(
                        kf[blk * MOBA_BLOCK:(blk + 1) * MOBA_BLOCK], axis=0, keepdims=True)


def _inproj(x, sc, sh, w_in):
    b, s, d = x.shape
    tm = min(512, s)
    n3 = 3 * BRANCH_WIDTH
    row = lambda bi, i: (bi, i, 0)
    vec = lambda bi, i: (bi, 0, 0)
    return pl.pallas_call(
        _inproj_kernel,
        grid=(b, s // tm),
        in_specs=[
            pl.BlockSpec((1, tm, d), row),
            pl.BlockSpec((1, 1, d), vec),
            pl.BlockSpec((1, 1, d), vec),
            pl.BlockSpec(w_in.shape, lambda bi, i: (0, 0)),
        ],
        out_specs=[
            pl.BlockSpec((1, tm, d), row),
            pl.BlockSpec((1, tm, n3), row),
            pl.BlockSpec((1, tm, n3), row),
            pl.BlockSpec((1, tm, n3), row),
            pl.BlockSpec((1, 1, tm // MOBA_BLOCK, BRANCH_WIDTH), lambda bi, i: (bi, i, 0, 0)),
        ],
        out_shape=[
            jax.ShapeDtypeStruct((b, s, d), BF16),
            jax.ShapeDtypeStruct((b, s, n3), BF16),
            jax.ShapeDtypeStruct((b, s, n3), F32),
            jax.ShapeDtypeStruct((b, s, n3), BF16),
            jax.ShapeDtypeStruct((b, s // tm, tm // MOBA_BLOCK, BRANCH_WIDTH), F32),
        ],
        compiler_params=_cparams(("parallel", "parallel")),
        name="inproj",
    )(x, sc, sh, w_in)


def _branches_kernel(z_ref, halo_ref, wp_ref, ps_ref, lg_ref, lb_ref, wsp_ref, bsp_ref,
                     ob_ref, oc_ref, *, tm):
    i = pl.program_id(1)
    w = BRANCH_WIDTH
    gd = w // len(POOL_WINDOWS)
    hw = POOL_MAX_WINDOW

    zp = z_ref[0, :, 0:w]
    halo = jnp.where(i > 0, halo_ref[0, :, 0:w], 0.0)
    ext = jnp.concatenate([halo, zp], axis=0)
    pos = (i * tm + lax.broadcasted_iota(jnp.int32, (tm, 1), 0) + 1).astype(F32)
    outs = []
    for g, win in enumerate(POOL_WINDOWS):
        e = ext[:, g * gd:(g + 1) * gd]
        shift = 1
        while shift < win:
            e = e + pltpu.roll(e, shift, axis=0)
            shift *= 2
        cnt = jnp.minimum(pos, float(win))
        pooled = e[hw:, :] / cnt - zp[:, g * gd:(g + 1) * gd]
        outs.append(jnp.dot(pooled.astype(BF16), wp_ref[g].astype(BF16), preferred_element_type=F32))
    ob_ref[0] = (jnp.concatenate(outs, axis=1) * ps_ref[...]).astype(BF16)

    u = jax.nn.gelu(z_ref[0, :, w:2 * w], approximate=True)
    v = _layer_norm(jax.nn.gelu(z_ref[0, :, 2 * w:3 * w], approximate=True), lg_ref[...], lb_ref[...])
    r = lax.broadcasted_iota(jnp.int32, (SGU_CHUNK, SGU_CHUNK), 0)
    c = lax.broadcasted_iota(jnp.int32, (SGU_CHUNK, SGU_CHUNK), 1)
    sg = w // SGU_GROUPS
    for g in range(SGU_GROUPS):
        wc = jnp.where(r >= c, wsp_ref[g], 0.0).astype(BF16)
        for ch in range(tm // SGU_CHUNK):
            rows = slice(ch * SGU_CHUNK, (ch + 1) * SGU_CHUNK)
            cols = slice(g * sg, (g + 1) * sg)
            vm = jnp.dot(wc, v[rows, cols].astype(BF16), preferred_element_type=F32) + bsp_ref[g]
            oc_ref[0, rows, cols] = (u[rows, cols] * vm).astype(BF16)


def _branches(zpuv, w_pool, pool_scale, ln_g, ln_b, w_sp, b_sp):
    b, s, n3 = zpuv.shape
    tm = min(512, s)
    w = BRANCH_WIDTH
    hw = POOL_MAX_WINDOW
    bsp = jnp.broadcast_to(b_sp[:, :, None], (SGU_GROUPS, SGU_CHUNK, w // SGU_GROUPS))
    full = lambda a: pl.BlockSpec(a.shape, lambda bi, i: (0,) * a.ndim)
    ps, lg, lb = pool_scale.reshape(1, w), ln_g.reshape(1, w), ln_b.reshape(1, w)
    return pl.pallas_call(
        functools.partial(_branches_kernel, tm=tm),
        grid=(b, s // tm),
        in_specs=[
            pl.BlockSpec((1, tm, n3), lambda bi, i: (bi, i, 0)),
            pl.BlockSpec((1, hw, n3), lambda bi, i: (bi, jnp.maximum(i * (tm // hw) - 1, 0), 0)),
            full(w_pool), full(ps), full(lg), full(lb), full(w_sp), full(bsp),
        ],
        out_specs=[pl.BlockSpec((1, tm, w), lambda bi, i: (bi, i, 0))] * 2,
        out_shape=[jax.ShapeDtypeStruct((b, s, w), BF16)] * 2,
        compiler_params=_cparams(("parallel", "parallel")),
        name="pool_sgu",
    )(zpuv, zpuv, w_pool, ps, lg, lb, w_sp, bsp)


_NT = (((1,), (1,)), ((), ()))


def _diag_keys(tq, c0):
    return c0 % tq + ROW_CHUNK


def _flash_tile(qa, ka_next, vb, s_ref, s_next, m_sc, acc_sc, patch=None, keys=None, keys_next=None):
    n_rows, tk = s_next.shape
    if vb is not None:
        lane = lax.broadcasted_iota(jnp.int32, vb.shape, 1)
        va = jnp.concatenate([vb, jnp.where(lane == 0, 1.0, 0.0).astype(BF16)], axis=1)
    for c0 in range(0, n_rows, ROW_CHUNK):
        rows = slice(c0, c0 + ROW_CHUNK)
        if vb is not None:
            if patch is not None:
                patch(c0)
            nk = tk if keys is None else keys(c0)
            s = s_ref[rows, :nk]
            m_old = m_sc[rows]
            m_new = jnp.maximum(m_old, jnp.max(s, axis=-1, keepdims=True))
            p = jnp.exp(s - m_new).astype(BF16)
            acc_sc[rows] = (jnp.exp(m_old - m_new) * acc_sc[rows]
                            + jnp.dot(p, va[:nk], preferred_element_type=F32))
            m_sc[rows] = m_new
        if ka_next is not None:
            nk = tk if keys_next is None else keys_next(c0)
            s_next[rows, :nk] = lax.dot_general(qa[rows], ka_next[:nk], _NT, preferred_element_type=F32)


def _near_patches(s_ref, bias_diag, bias_prev, tq, diagonal, c0):
    t = BIAS_TILE
    n_sub = tq // t
    hh = c0 // tq
    cols = lambda a: slice(a * t, (a + 1) * t)
    for a in range((c0 % tq) // t, (c0 % tq + ROW_CHUNK) // t):
        rows = slice(hh * tq + a * t, hh * tq + (a + 1) * t)
        if not diagonal:
            if a == 0:
                s_ref[rows, cols(n_sub - 1)] = s_ref[rows, cols(n_sub - 1)] + bias_prev(hh)
            continue
        s_ref[rows, cols(a)] = s_ref[rows, cols(a)] + bias_diag(hh)
        if a >= 1:
            s_ref[rows, cols(a - 1)] = s_ref[rows, cols(a - 1)] + bias_prev(hh)
        hi = _diag_keys(tq, c0)
        if (a + 1) * t < hi:
            s_ref[rows, (a + 1) * t:hi] = jnp.full((t, hi - (a + 1) * t), MASK_VALUE, F32)


def _run_causal_tiles(i, tq, qa, keys_of, values_of, s2_ref, m_sc, acc_sc, bias_diag, bias_prev):
    diag_keys = functools.partial(_diag_keys, tq)
    s0, s1 = s2_ref.at[0], s2_ref.at[1]
    tile = functools.partial(_flash_tile, qa, m_sc=m_sc, acc_sc=acc_sc)
    n_far = jnp.maximum(i - 1, 0)

    tile(keys_of(i), None, None, s0, keys_next=diag_keys)
    tile(keys_of(jnp.maximum(i - 1, 0)), values_of(i), s0, s1, keys=diag_keys,
         patch=functools.partial(_near_patches, s0, bias_diag, bias_prev, tq, True))

    @pl.when(i >= 1)
    def _():
        tile(keys_of(0), values_of(i - 1), s1, s0,
             patch=functools.partial(_near_patches, s1, bias_diag, bias_prev, tq, False))

    def far_pair(k, carry):
        tile(keys_of(2 * k + 1), values_of(2 * k), s0, s1)
        tile(keys_of(2 * k + 2), values_of(2 * k + 1), s1, s0)
        return carry

    n_pairs = jnp.maximum(n_far - 1, 0) // 2
    lax.fori_loop(0, n_pairs, far_pair, 0)
    first_left = 2 * n_pairs

    @pl.when(n_far - first_left == 2)
    def _():
        tile(keys_of(first_left + 1), values_of(first_left), s0, s1)
        tile(None, values_of(first_left + 1), s1, s1)

    @pl.when(n_far - first_left == 1)
    def _():
        tile(None, values_of(first_left), s0, s0)


def _attn_scratch(tq):
    return [pltpu.VMEM((2, 2 * tq, tq), F32), pltpu.VMEM((2 * tq, 1), F32), pltpu.VMEM((2 * tq, 2 * LANES), F32)]


def _diff_attn_kernel(lam_ref, g_ref, q_ref, k_ref, v_ref, bd_ref, bp_ref, o_ref,
                      s_ref, m_sc, acc_sc, *, tq, lam_init):
    i = pl.program_id(2)
    q = q_ref[0]
    lane = lax.broadcasted_iota(jnp.int32, (1, LANES), 1)
    zero = jnp.zeros_like(q)
    qq = jnp.concatenate([jnp.where(lane < HEAD_DIM, q, zero),
                          jnp.where(lane >= HEAD_DIM, q, zero)], axis=0)
    m_sc[...] = jnp.full(m_sc.shape, MASK_VALUE, F32)
    acc_sc[...] = jnp.zeros(acc_sc.shape, F32)

    tile = lambda ref, j: ref[0, pl.ds(pl.multiple_of(j * tq, tq), tq), :]
    _run_causal_tiles(i, tq, qq, functools.partial(tile, k_ref), functools.partial(tile, v_ref),
                      s_ref, m_sc, acc_sc, lambda hh: bd_ref[0], lambda hh: bp_ref[0])

    acc = acc_sc[...]
    o = acc[:, :LANES] / acc[:, LANES:LANES + 1]
    lam = (jnp.exp(jnp.sum(lam_ref[0:1, :] * lam_ref[1:2, :], axis=-1, keepdims=True))
           - jnp.exp(jnp.sum(lam_ref[2:3, :] * lam_ref[3:4, :], axis=-1, keepdims=True)) + lam_init)
    od = o[:tq] - lam * o[tq:]
    ms = jnp.mean(jnp.square(od), -1, keepdims=True)
    o_ref[0] = ((od * lax.rsqrt(ms + NORM_EPS) * g_ref[...]) * (1.0 - lam_init)).astype(BF16)


def _diff_attention(qkv, lams, norm_g, bias_diag, bias_prev, lam_init):
    b, s, _ = qkv.shape
    tq = min(ATTN_SUPER, s)
    t = BIAS_TILE
    nh = DIFF_HEADS
    return pl.pallas_call(
        functools.partial(_diff_attn_kernel, tq=tq, lam_init=lam_init),
        grid=(b, nh, s // tq),
        in_specs=[
            pl.BlockSpec(lams.shape, lambda bi, h, i: (0, 0)),
            pl.BlockSpec(norm_g.shape, lambda bi, h, i: (0, 0)),
            pl.BlockSpec((1, tq, LANES), lambda bi, h, i: (bi, i, h)),
            pl.BlockSpec((1, s, LANES), lambda bi, h, i: (bi, 0, nh + h)),
            pl.BlockSpec((1, s, LANES), lambda bi, h, i: (bi, 0, 2 * nh + h)),
            pl.BlockSpec((1, t, t), lambda bi, h, i: (h, 0, 0)),
            pl.BlockSpec((1, t, t), lambda bi, h, i: (h, 0, 0)),
        ],
        out_specs=pl.BlockSpec((1, tq, LANES), lambda bi, h, i: (bi, i, h)),
        out_shape=jax.ShapeDtypeStruct((b, s, BRANCH_WIDTH), BF16),
        scratch_shapes=_attn_scratch(tq),
        compiler_params=_cparams(("parallel", "parallel", "arbitrary")),
        name="diff_attention",
    )(lams, norm_g, qkv, qkv, qkv, bias_diag, bias_prev)


def _moba_kernel(q_ref, km_ref, k_ref, v_ref, bd_ref, bp_ref, o_ref,
                 qa_sc, s_ref, m_sc, acc_sc, *, tq):
    i = pl.program_id(2)
    shift = int(math.log2(MOBA_BLOCK))
    q = q_ref[0]
    lane = lax.broadcasted_iota(jnp.int32, (1, LANES), 1)
    row = lax.broadcasted_iota(jnp.int32, (tq, 1), 0)
    n_slot = -(-(k_ref.shape[1] // MOBA_BLOCK) // 8) * 8
    blk = lax.broadcasted_iota(jnp.int32, (n_slot, tq), 0).astype(F32)
    own = lax.shift_right_logical(i * tq + lax.broadcasted_iota(jnp.int32, (1, tq), 1),
                                  shift).astype(F32)
    zero = jnp.zeros_like(q)
    km = km_ref[0, :n_slot, :].astype(BF16)
    ninf = jnp.float32(-jnp.inf)
    for hh in range(2):
        qh = jnp.where((lane >= hh * HEAD_DIM) & (lane < (hh + 1) * HEAD_DIM), q, zero)
        gate = lax.dot_general(km, qh, _NT, preferred_element_type=F32)
        g = jnp.where(blk < own, gate, ninf)
        sel = blk >= own
        for _ in range(MOBA_TOPK):
            mx = jnp.max(g, axis=0, keepdims=True)
            first = jnp.min(jnp.where(g == mx, blk, float(LANES)), axis=0, keepdims=True)
            hit = blk == first
            sel = sel | (hit & (mx > ninf))
            g = jnp.where(hit, ninf, g)
        bias = jnp.concatenate([jnp.where(sel, 0.0, MASK_VALUE), jnp.zeros((LANES - n_slot, tq), F32)], axis=0)
        qa_sc[hh * tq:(hh + 1) * tq, :] = jnp.concatenate([qh, bias.T.astype(BF16)], axis=1)
    m_sc[...] = jnp.full(m_sc.shape, MASK_VALUE, F32)
    acc_sc[...] = jnp.zeros(acc_sc.shape, F32)

    def keys_of(j):
        start = pl.multiple_of(j * tq, tq)
        kblk = lax.shift_right_logical(start + row, shift)
        return jnp.concatenate([k_ref[0, pl.ds(start, tq), :],
                                jnp.where(lane == kblk, 1.0, 0.0).astype(BF16)], axis=1)

    values_of = lambda j: v_ref[0, pl.ds(pl.multiple_of(j * tq, tq), tq), :]
    _run_causal_tiles(i, tq, qa_sc, keys_of, values_of, s_ref, m_sc, acc_sc,
                      lambda hh: bd_ref[0, hh], lambda hh: bp_ref[0, hh])

    acc = acc_sc[...]
    o = acc[:, :LANES] / acc[:, LANES:LANES + 1]
    o_ref[0] = jnp.where(lane < HEAD_DIM, o[:tq], o[tq:]).astype(BF16)


def _moba_attention(qkv, kmean, bias_diag, bias_prev):
    b, s, _ = qkv.shape
    tq = min(ATTN_SUPER, s)
    t = BIAS_TILE
    npair = MOBA_HEADS // 2
    return pl.pallas_call(
        functools.partial(_moba_kernel, tq=tq),
        grid=(b, npair, s // tq),
        in_specs=[
            pl.BlockSpec((1, tq, LANES), lambda bi, p, i: (bi, i, p)),
            pl.BlockSpec((1, LANES, LANES), lambda bi, p, i: (bi, 0, p)),
            pl.BlockSpec((1, s, LANES), lambda bi, p, i: (bi, 0, npair + p)),
            pl.BlockSpec((1, s, LANES), lambda bi, p, i: (bi, 0, 2 * npair + p)),
            pl.BlockSpec((1, 2, t, t), lambda bi, p, i: (p, 0, 0, 0)),
            pl.BlockSpec((1, 2, t, t), lambda bi, p, i: (p, 0, 0, 0)),
        ],
        out_specs=pl.BlockSpec((1, tq, LANES), lambda bi, p, i: (bi, i, p)),
        out_shape=jax.ShapeDtypeStruct((b, s, BRANCH_WIDTH), BF16),
        scratch_shapes=[pltpu.VMEM((2 * tq, 2 * LANES), BF16)] + _attn_scratch(tq),
        compiler_params=_cparams(("parallel", "parallel", "arbitrary")),
        name="moba_attention",
    )(qkv, kmean, qkv, qkv, bias_diag, bias_prev)


def _merge_kernel(h_ref, oa_ref, ob_ref, oc_ref, od_ref, wg_ref, bg_ref, wb_ref, out_ref):
    h = h_ref[...]
    acc = None
    for br, o_ref in enumerate((oa_ref, ob_ref, oc_ref, od_ref)):
        gate = _sigmoid(jnp.dot(h, wg_ref[br], preferred_element_type=F32) + bg_ref[br])
        term = gate * jnp.dot(o_ref[...], wb_ref[br], preferred_element_type=F32)
        acc = term if acc is None else acc + term
    out_ref[...] = acc.astype(BF16)


def _merge(h, branches, w_gate, b_gate, w_branch):
    t, d = h.shape
    w = BRANCH_WIDTH
    tm = min(512, t)
    tn = 512
    nbr = len(branches)
    row = lambda j, i: (i, 0)
    return pl.pallas_call(
        _merge_kernel,
        grid=(d // tn, t // tm),
        in_specs=[pl.BlockSpec((tm, d), row)] + [pl.BlockSpec((tm, w), row)] * nbr + [
            pl.BlockSpec((nbr, d, tn), lambda j, i: (0, 0, j)),
            pl.BlockSpec((nbr, 1, tn), lambda j, i: (0, 0, j)),
            pl.BlockSpec((nbr, w, tn), lambda j, i: (0, 0, j)),
        ],
        out_specs=pl.BlockSpec((tm, tn), lambda j, i: (i, j)),
        out_shape=jax.ShapeDtypeStruct((t, d), BF16),
        compiler_params=_cparams(("parallel", "parallel")),
        name="gated_merge",
    )(h, *branches, w_gate, b_gate, w_branch)


def _route(aff, sel):
    ng, pg = N_EXPERT_GROUPS, EXPERTS_PER_GROUP
    srow = [sel[e:e + 1] for e in range(N_EXPERTS)]
    arow = [aff[e:e + 1] for e in range(N_EXPERTS)]
    best_score, best_grp = None, None
    for g in range(ng):
        rows = srow[g * pg:(g + 1) * pg]
        score = None
        for a in range(pg):
            for b in range(a + 1, pg):
                pair = rows[a] + rows[b]
                score = pair if score is None else jnp.maximum(score, pair)
        if best_score is None:
            best_score, best_grp = score, jnp.zeros_like(score, dtype=jnp.int32)
        else:
            better = score > best_score
            best_score = jnp.where(better, score, best_score)
            best_grp = jnp.where(better, g, best_grp)

    def pick(rows_by_group):
        out = rows_by_group[0]
        for g in range(1, ng):
            out = jnp.where(best_grp == g, rows_by_group[g], out)
        return out

    cand_s = [pick([srow[g * pg + k] for g in range(ng)]) for k in range(pg)]
    cand_a = [pick([arow[g * pg + k] for g in range(ng)]) for k in range(pg)]

    def argmax_first(vals, exclude):
        bv, bi, ba = None, None, None
        for k in range(pg):
            v = vals[k] if exclude is None else jnp.where(exclude == k, -jnp.inf, vals[k])
            if bv is None:
                bv, bi, ba = v, jnp.zeros_like(best_grp), cand_a[k]
            else:
                better = v > bv
                bv = jnp.where(better, v, bv)
                bi = jnp.where(better, k, bi)
                ba = jnp.where(better, cand_a[k], ba)
        return bi, ba

    i0, a0 = argmax_first(cand_s, None)
    i1, a1 = argmax_first(cand_s, i0)
    tot = a0 + a1
    return best_grp * pg + i0, best_grp * pg + i1, a0 / tot, a1 / tot


def _outproj_kernel(m_ref, x_ref, wo_ref, g1_ref, lg_ref, lb_ref, sc2_ref, sh2_ref, wr_ref, br_ref,
                    x1_ref, h2_ref, ri_ref, rw_ref, cnt_ref, *, tm):
    y = jnp.dot(m_ref[0], wo_ref[...], preferred_element_type=F32)
    x1 = _layer_norm(ALPHA * x_ref[0] + (1.0 + g1_ref[0]) * y, lg_ref[...], lb_ref[...])
    x1_ref[0] = x1
    h2 = x1 * (1.0 + sc2_ref[0]) + sh2_ref[0]
    h2_ref[0] = h2
    logits = lax.dot_general(wr_ref[...], h2.astype(BF16), _NT, preferred_element_type=F32)
    aff = _sigmoid(logits)
    e0, e1, w0, w1 = _route(aff, aff + br_ref[...])

    @pl.when((pl.program_id(0) == 0) & (pl.program_id(1) == 0))
    def _():
        cnt_ref[...] = jnp.zeros(cnt_ref.shape, F32)

    erow = lax.broadcasted_iota(jnp.int32, (N_EXPERTS, tm), 0)
    hit0, hit1 = erow == e0, erow == e1
    hits = jnp.where(hit0 | hit1, 1.0, 0.0)
    earlier = (lax.broadcasted_iota(jnp.int32, (tm, tm), 0)
               < lax.broadcasted_iota(jnp.int32, (tm, tm), 1)).astype(BF16)
    before = jnp.dot(hits.astype(BF16), earlier, preferred_element_type=F32) + cnt_ref[:, 0:1]
    rank = lambda hit: jnp.sum(jnp.where(hit, before, 0.0), axis=0, keepdims=True).astype(jnp.int32)
    cnt_ref[...] = cnt_ref[...] + jnp.sum(hits, axis=1, keepdims=True)
    ri_ref[0] = jnp.concatenate([e0, e1, rank(hit0), rank(hit1), jnp.zeros((4, tm), jnp.int32)], axis=0)
    rw_ref[0] = jnp.concatenate([w0, w1, jnp.zeros((6, tm), F32)], axis=0)


def _outproj(merged, x, w_out, g1, ln_g, ln_b, sc2, sh2, w_router_t, b_router):
    b, s, d = x.shape
    tm = min(512, s)
    row = lambda bi, i: (bi, i, 0)
    vec = lambda bi, i: (bi, 0, 0)
    full = lambda a: pl.BlockSpec(a.shape, lambda bi, i: (0,) * a.ndim)
    lg, lb = ln_g.reshape(1, d), ln_b.reshape(1, d)
    br = b_router.reshape(N_EXPERTS, 1)
    return pl.pallas_call(
        functools.partial(_outproj_kernel, tm=tm),
        grid=(b, s // tm),
        in_specs=[
            pl.BlockSpec((1, tm, d), row), pl.BlockSpec((1, tm, d), row), full(w_out),
            pl.BlockSpec((1, 1, d), vec), full(lg), full(lb),
            pl.BlockSpec((1, 1, d), vec), pl.BlockSpec((1, 1, d), vec),
            full(w_router_t), full(br),
        ],
        out_specs=[
            pl.BlockSpec((1, tm, d), row), pl.BlockSpec((1, tm, d), row),
            pl.BlockSpec((1, 8, tm), lambda bi, i: (bi, 0, i)),
            pl.BlockSpec((1, 8, tm), lambda bi, i: (bi, 0, i)),
            pl.BlockSpec((N_EXPERTS, LANES), lambda bi, i: (0, 0)),
        ],
        out_shape=[
            jax.ShapeDtypeStruct((b, s, d), F32), jax.ShapeDtypeStruct((b, s, d), F32),
            jax.ShapeDtypeStruct((b, 8, s), jnp.int32), jax.ShapeDtypeStruct((b, 8, s), F32),
            jax.ShapeDtypeStruct((N_EXPERTS, LANES), F32),
        ],
        compiler_params=_cparams(("arbitrary", "arbitrary")),
        name="outproj_ln_router",
    )(merged.reshape(b, s, d), x, w_out, g1, lg, lb, sc2, sh2, w_router_t, br)


def _row_copy(src, src_row, dst, dst_row, sem):
    return pltpu.make_async_copy(src.at[pl.ds(src_row, 1)], dst.at[pl.ds(dst_row, 1)], sem)


def _dispatch_kernel(tail_ref, nu_ref, dest_ref, h2_ref, xs_ref, zeros_sc, sem, zsem, *, rows):
    @pl.when(pl.program_id(0) == 0)
    def _():
        zeros_sc[...] = jnp.zeros(zeros_sc.shape, F32)
        fill = lambda row0: pltpu.make_async_copy(
            zeros_sc, xs_ref.at[pl.ds(pl.multiple_of(row0, SLOT_BLOCK), SLOT_BLOCK)], zsem)
        unused = (nu_ref[0], xs_ref.shape[0] // SLOT_BLOCK)
        for e in range(N_EXPERTS):
            pl.when(tail_ref[e] >= 0)(lambda e=e: fill(tail_ref[e]).start())
        lax.fori_loop(*unused, lambda b, c: (fill(b * SLOT_BLOCK).start(), c)[1], 0)
        for e in range(N_EXPERTS):
            pl.when(tail_ref[e] >= 0)(lambda e=e: fill(tail_ref[e]).wait())
        lax.fori_loop(*unused, lambda b, c: (fill(b * SLOT_BLOCK).wait(), c)[1], 0)

    def issue(r, carry):
        for k in range(TOP_K):
            _row_copy(h2_ref, r, xs_ref, dest_ref[0, 0, TOP_K * r + k], sem).start()
        return carry

    lax.fori_loop(0, rows, issue, 0, unroll=4)
    for k in range(TOP_K):
        pltpu.make_async_copy(h2_ref, xs_ref.at[pl.ds(0, rows)], sem).wait()


def _dispatch(dest, tail_start, n_used, h2, n_slots):
    t, d = h2.shape
    rows = min(512, t)
    return pl.pallas_call(
        functools.partial(_dispatch_kernel, rows=rows),
        grid_spec=pltpu.PrefetchScalarGridSpec(
            num_scalar_prefetch=2,
            grid=(t // rows,),
            in_specs=[
                pl.BlockSpec((1, 1, TOP_K * rows), lambda i, tail, nu: (i, 0, 0), memory_space=pltpu.SMEM),
                pl.BlockSpec((rows, d), lambda i, tail, nu: (i, 0)),
            ],
            out_specs=pl.BlockSpec(memory_space=pl.ANY),
            scratch_shapes=[pltpu.VMEM((SLOT_BLOCK, d), F32), pltpu.SemaphoreType.DMA(()),
                            pltpu.SemaphoreType.DMA(())],
        ),
        out_shape=jax.ShapeDtypeStruct((n_slots, d), F32),
        compiler_params=_cparams(("arbitrary",)),
        name="moe_dispatch",
    )(tail_start, n_used, dest.reshape(t // rows, 1, TOP_K * rows), h2)


def _cast_kernel(w_ref, o_ref):
    o_ref[0] = w_ref[0, 0].astype(BF16)


def _cast_layer(w, layer, rows):
    _, e, r, c = w.shape
    return pl.pallas_call(
        _cast_kernel,
        grid=(e, r // rows),
        in_specs=[pl.BlockSpec((1, 1, rows, c), lambda ei, i: (layer, ei, i, 0))],
        out_specs=pl.BlockSpec((1, rows, c), lambda ei, i: (ei, i, 0)),
        out_shape=jax.ShapeDtypeStruct((e, r, c), BF16),
        compiler_params=_cparams(("parallel", "parallel")),
        name="expert_weight_cast",
    )(w)


def _expert_kernel(be_ref, nu_ref, xs_ref, wg_ref, wu_ref, wd_ref, ys_ref):
    del be_ref
    used = pl.program_id(0) < nu_ref[0]

    @pl.when(used)
    def _():
        xb = xs_ref[...].astype(BF16)
        g = jnp.dot(xb, wg_ref[0], preferred_element_type=F32)
        u = jnp.dot(xb, wu_ref[0], preferred_element_type=F32)
        act = ((g * _sigmoid(g)) * u).astype(BF16)
        ys_ref[...] = jnp.dot(act, wd_ref[0], preferred_element_type=F32)

    @pl.when(jnp.logical_not(used))
    def _():
        ys_ref[...] = jnp.zeros(ys_ref.shape, F32)


def _expert_ffn(blk_expert, n_used, xs, w_g, w_u, w_d):
    n_slots, d = xs.shape
    f = w_d.shape[1]
    blk = SLOT_BLOCK
    last = lambda b, nu: jnp.minimum(b, nu[0] - 1)
    wspec = lambda shape: pl.BlockSpec(shape, lambda b, be, nu: (be[last(b, nu)], 0, 0),
                                       pipeline_mode=pl.Buffered(1))
    return pl.pallas_call(
        _expert_kernel,
        grid_spec=pltpu.PrefetchScalarGridSpec(
            num_scalar_prefetch=2,
            grid=(n_slots // blk,),
            in_specs=[
                pl.BlockSpec((blk, d), lambda b, be, nu: (last(b, nu), 0)),
                wspec((1, d, f)), wspec((1, d, f)), wspec((1, f, d)),
            ],
            out_specs=pl.BlockSpec((blk, d), lambda b, be, nu: (b, 0)),
        ),
        out_shape=jax.ShapeDtypeStruct((n_slots, d), F32),
        compiler_params=_cparams(("arbitrary",)),
        name="moe_experts",
    )(blk_expert, n_used, xs, w_g, w_u, w_d)


def _combine_kernel(dest_ref, dest_next_ref, w_ref, x1_ref, g2_ref, lg_ref, lb_ref, ys_ref, o_ref,
                    buf, sem, *, rows):
    step = pl.program_id(0) * pl.num_programs(1) + pl.program_id(1)
    n_steps = pl.num_programs(0) * pl.num_programs(1)
    slot = step % 2

    def gather(dref, sl):
        def issue(r, carry):
            for k in range(TOP_K):
                _row_copy(ys_ref, dref[0, 0, TOP_K * r + k], buf.at[sl, k], r, sem.at[sl]).start()
            return carry
        lax.fori_loop(0, rows, issue, 0, unroll=4)

    pl.when(step == 0)(lambda: gather(dest_ref, 0))
    pl.when(step + 1 < n_steps)(lambda: gather(dest_next_ref, 1 - slot))
    for k in range(TOP_K):
        pltpu.make_async_copy(ys_ref.at[pl.ds(0, rows)], buf.at[slot, k], sem.at[slot]).wait()
    y = buf[slot, 0] * w_ref[:, 0:1] + buf[slot, 1] * w_ref[:, 1:2]
    o_ref[0] = _layer_norm(ALPHA * x1_ref[0] + (1.0 + g2_ref[0]) * y, lg_ref[...], lb_ref[...])


def _combine(dest, wts, x1, g2, ln_g, ln_b, ys):
    b, s, d = x1.shape
    rows = min(256, s)
    nt = s // rows
    lg, lb = ln_g.reshape(1, d), ln_b.reshape(1, d)
    full = lambda a: pl.BlockSpec(a.shape, lambda bi, i: (0,) * a.ndim)
    dest3 = dest.reshape(b * nt, 1, TOP_K * rows)
    smem = lambda ahead: pl.BlockSpec(
        (1, 1, TOP_K * rows), lambda bi, i: (jnp.minimum(bi * nt + i + ahead, b * nt - 1), 0, 0),
        memory_space=pltpu.SMEM)
    return pl.pallas_call(
        functools.partial(_combine_kernel, rows=rows),
        grid=(b, nt),
        in_specs=[
            smem(0), smem(1),
            pl.BlockSpec((rows, TOP_K), lambda bi, i: (bi * nt + i, 0)),
            pl.BlockSpec((1, rows, d), lambda bi, i: (bi, i, 0)),
            pl.BlockSpec((1, 1, d), lambda bi, i: (bi, 0, 0)),
            full(lg), full(lb),
            pl.BlockSpec(memory_space=pl.ANY),
        ],
        out_specs=pl.BlockSpec((1, rows, d), lambda bi, i: (bi, i, 0)),
        out_shape=jax.ShapeDtypeStruct((b, s, d), F32),
        scratch_shapes=[pltpu.VMEM((2, TOP_K, rows, d), F32), pltpu.SemaphoreType.DMA((2,))],
        compiler_params=_cparams(("arbitrary", "arbitrary")),
        name="moe_combine_ln",
    )(dest3, dest3, wts, x1, g2, lg, lb, ys)


def _slot_plan(e_flat, rank, counts, n_slots):
    blk = SLOT_BLOCK
    padded = (counts + blk - 1) // blk * blk
    pend = jnp.cumsum(padded)
    experts = jnp.arange(N_EXPERTS, dtype=jnp.int32)
    dest = jnp.sum(jnp.where(e_flat[:, None] == experts[None, :], (pend - padded)[None, :], 0), axis=1) + rank
    nblk = n_slots // blk
    blk_expert = jnp.minimum(
        jnp.searchsorted(pend, jnp.arange(nblk, dtype=jnp.int32) * blk, side='right'), N_EXPERTS - 1)
    tail_start = jnp.where(padded > 0, pend - blk, -1)
    return (dest.astype(jnp.int32), blk_expert.astype(jnp.int32), (pend[-1:] // blk).astype(jnp.int32),
            tail_start.astype(jnp.int32))


def _t5_bucket(dist):
    max_exact = REL_BUCKETS // 2
    d = jnp.maximum(dist, 1).astype(F32)
    large = max_exact + (jnp.log(d / max_exact) / math.log(REL_MAX_DIST / max_exact)
                         * (REL_BUCKETS - max_exact)).astype(jnp.int32)
    large = jnp.minimum(large, REL_BUCKETS - 1)
    return jnp.where(dist < max_exact, dist, large)


def _bias_tiles(rel_table, t):
    r = jnp.arange(t, dtype=jnp.int32)[:, None]
    c = jnp.arange(t, dtype=jnp.int32)[None, :]
    dd = r - c

    def lookup(dist):
        onehot = (_t5_bucket(dist)[:, :, None] == jnp.arange(REL_BUCKETS, dtype=jnp.int32)).astype(F32)
        return jnp.einsum('rck,kh->hrc', onehot, rel_table, precision=lax.Precision.HIGHEST)

    diag = jnp.where(dd >= 0, lookup(jnp.maximum(dd, 0)), MASK_VALUE)
    return diag, lookup(dd + t)


def kernel(x, c, rel_bias, w_router, b_router, w_ada, b_ada, w_in, lam_q1, lam_k1, lam_q2, lam_k2,
           diff_norm_g, w_pool, pool_scale, sgu_ln_g, sgu_ln_b, w_spatial, b_spatial, w_branch,
           w_gate, b_gate, w_out, ln1_g, ln1_b, w_e_gate, w_e_up, w_e_down, ln2_g, ln2_b):
    b, s, d = x.shape
    t = b * s
    assert d == D_MODEL and s % MOBA_BLOCK == 0 and s // MOBA_BLOCK <= LANES
    assert int(16 * 8 ** (15 / 16)) + 1 < MOBA_BLOCK and REL_BUCKETS == 32 and REL_MAX_DIST == 128

    assert s % min(ATTN_SUPER, s) == 0 and BIAS_TILE == MOBA_BLOCK
    diag, prev = _bias_tiles(rel_bias - rel_bias[REL_BUCKETS - 1][None, :], BIAS_TILE)
    a_diag, a_prev = diag[:DIFF_HEADS], prev[:DIFF_HEADS]
    pair = lambda a: a[DIFF_HEADS:].reshape(MOBA_HEADS // 2, 2, BIAS_TILE, BIAS_TILE)
    d_diag, d_prev = pair(diag), pair(prev)

    mod = _ada_mod(c, w_ada, b_ada)
    w_router_t = w_router.T.astype(BF16)
    n_slots = (-(-t * TOP_K // SLOT_BLOCK) + N_EXPERTS) * SLOT_BLOCK

    for layer in range(DEPTH):
        sh1, sc1, g1, sh2, sc2, g2 = (m[:, None, :] for m in jnp.split(mod[layer], 6, axis=-1))
        lam_init = 0.8 - 0.6 * math.exp(-0.3 * layer)
        lams = jnp.stack([lam_q1[layer], lam_k1[layer], lam_q2[layer], lam_k2[layer]])

        h, qkv_a, zpuv, qkv_d, kmean = _inproj(x, sc1, sh1, w_in[layer].astype(BF16))
        o_b, o_c = _branches(zpuv, w_pool[layer], pool_scale[layer], sgu_ln_g[layer], sgu_ln_b[layer],
                             w_spatial[layer], b_spatial[layer])
        o_a = _diff_attention(qkv_a, lams, diff_norm_g[layer].reshape(1, 2 * HEAD_DIM),
                              a_diag, a_prev, lam_init)
        kmean = kmean.reshape(b, s // MOBA_BLOCK, BRANCH_WIDTH)
        kmean = jnp.pad(kmean, ((0, 0), (0, LANES - s // MOBA_BLOCK), (0, 0)))
        o_d = _moba_attention(qkv_d, kmean, d_diag, d_prev)

        flat = lambda a: a.reshape(t, a.shape[-1])
        merged = _merge(flat(h), [flat(o_a), flat(o_b), flat(o_c), flat(o_d)],
                        w_gate[layer].astype(BF16), b_gate[layer][:, None, :], w_branch[layer].astype(BF16))
        x1, h2, ridx, rwts, counts = _outproj(
            merged, x, w_out[layer].astype(BF16), g1, ln1_g[layer], ln1_b[layer], sc2, sh2, w_router_t, b_router)

        per_assignment = lambda rows: jnp.transpose(rows, (0, 2, 1)).reshape(t * TOP_K)
        wts = per_assignment(rwts[:, :TOP_K, :]).reshape(t, TOP_K)
        dest, blk_expert, n_used, tail_start = _slot_plan(
            per_assignment(ridx[:, :TOP_K, :]), per_assignment(ridx[:, TOP_K:2 * TOP_K, :]),
            counts[:, 0].astype(jnp.int32), n_slots)
        xs = _dispatch(dest, tail_start, n_used, h2.reshape(t, d), n_slots)
        ys = _expert_ffn(blk_expert, n_used, xs, _cast_layer(w_e_gate, layer, 512),
                         _cast_layer(w_e_up, layer, 512), _cast_layer(w_e_down, layer, EXPERT_FF // 4))
        x = _combine(dest, wts, x1, g2, ln2_g[layer], ln2_b[layer], ys)
    return x
```

```python
import functools
import math

import jax
import jax.numpy as jnp
from jax import lax
from jax.experimental import pallas as pl
from jax.experimental.pallas import tpu as pltpu

F32 = jnp.float32
BF16 = jnp.bfloat16

D_MODEL = 2048
DEPTH = 2
BRANCH_WIDTH = 512
HEAD_DIM = 64
N_SECTIONS = 9
DIFF_HEADS = 4
POOL_WINDOWS = (2, 4, 8, 16)
POOL_MAX_WINDOW = 16
SGU_CHUNK = 128
SGU_GROUPS = 4
MOBA_HEADS = 8
MOBA_BLOCK = 256
MOBA_TOPK = 3
REL_BUCKETS = 32
REL_MAX_DIST = 128
N_EXPERTS = 16
N_EXPERT_GROUPS = 4
EXPERTS_PER_GROUP = 4
TOP_K = 2
EXPERT_FF = 1408
ALPHA = (2 * DEPTH) ** 0.25
NORM_EPS = 1e-5

LANES = 128
MASK_VALUE = -1e30
ATTN_Q_TILE = 1024
ATTN_KV_TILE = 1024
LOGIT_BUFFERS = 2
BIAS_TILE = 256
ROW_CHUNK = 512
SLOT_BLOCK = 512
OUTPROJ_ROW_CHUNK = 512
VMEM_LIMIT = 56 * 2 ** 20
ATTN_VMEM_LIMIT = 62 * 2 ** 20


def _cparams(semantics, vmem=VMEM_LIMIT):
    return pltpu.CompilerParams(dimension_semantics=semantics, vmem_limit_bytes=vmem)


def _sigmoid(x):
    return 1.0 / (1.0 + jnp.exp(-x))


def _layer_norm(r, g, b):
    mu = jnp.mean(r, -1, keepdims=True)
    var = jnp.mean(jnp.square(r - mu), -1, keepdims=True)
    return (r - mu) * lax.rsqrt(var + NORM_EPS) * g + b


def _ada_kernel(ct_ref, w_ref, b_ref, o_ref, *, batch):
    ct = ct_ref[...]
    ca = ct * _sigmoid(ct)
    w = w_ref[0]
    rows = [jnp.sum(w * ca[:, b:b + 1], axis=0, keepdims=True) for b in range(batch)]
    o_ref[0] = jnp.concatenate(rows, axis=0) + b_ref[0]


def _ada_mod(c, w_ada, b_ada):
    batch, d = c.shape
    depth, _, n = w_ada.shape
    tn = 512
    return pl.pallas_call(
        functools.partial(_ada_kernel, batch=batch),
        grid=(depth, n // tn),
        in_specs=[
            pl.BlockSpec((d, batch), lambda l, j: (0, 0)),
            pl.BlockSpec((1, d, tn), lambda l, j: (l, 0, j)),
            pl.BlockSpec((1, 1, tn), lambda l, j: (l, 0, j)),
        ],
        out_specs=pl.BlockSpec((1, batch, tn), lambda l, j: (l, 0, j)),
        out_shape=jax.ShapeDtypeStruct((depth, batch, n), F32),
        compiler_params=_cparams(("parallel", "parallel")),
        name="ada_mod",
    )(c.T, w_ada, b_ada.reshape(depth, 1, n))


def _inproj_kernel(x_ref, sc_ref, sh_ref, w_ref, h_ref, a_ref, p_ref, d_ref, km_ref):
    w = BRANCH_WIDTH
    h = (x_ref[0] * (1.0 + sc_ref[0]) + sh_ref[0]).astype(BF16)
    h_ref[0] = h
    for s in range(N_SECTIONS):
        z = jnp.dot(h, w_ref[:, s * w:(s + 1) * w], preferred_element_type=F32)
        if s in (0, 6):
            z = z * (HEAD_DIM ** -0.5)
        if s < 3:
            a_ref[0, :, s * w:(s + 1) * w] = z.astype(BF16)
        elif s < 6:
            p_ref[0, :, (s - 3) * w:(s - 2) * w] = z
        else:
            zb = z.astype(BF16)
            d_ref[0, :, (s - 6) * w:(s - 5) * w] = zb
            if s == 7:
                kf = zb.astype(F32)
                for blk in range(kf.shape[0] // MOBA_BLOCK):
                    km_ref[0, 0, blk:blk + 1, :] = jnp.mean(
                        kf[blk * MOBA_BLOCK:(blk + 1) * MOBA_BLOCK], axis=0, keepdims=True)


def _inproj(x, sc, sh, w_in):
    b, s, d = x.shape
    tm = min(512, s)
    n3 = 3 * BRANCH_WIDTH
    row = lambda bi, i: (bi, i, 0)
    vec = lambda bi, i: (bi, 0, 0)
    return pl.pallas_call(
        _inproj_kernel,
        grid=(b, s // tm),
        in_specs=[
            pl.BlockSpec((1, tm, d), row),
            pl.BlockSpec((1, 1, d), vec),
            pl.BlockSpec((1, 1, d), vec),
            pl.BlockSpec(w_in.shape, lambda bi, i: (0, 0)),
        ],
        out_specs=[
            pl.BlockSpec((1, tm, d), row),
            pl.BlockSpec((1, tm, n3), row),
            pl.BlockSpec((1, tm, n3), row),
            pl.BlockSpec((1, tm, n3), row),
            pl.BlockSpec((1, 1, tm // MOBA_BLOCK, BRANCH_WIDTH), lambda bi, i: (bi, i, 0, 0)),
        ],
        out_shape=[
            jax.ShapeDtypeStruct((b, s, d), BF16),
            jax.ShapeDtypeStruct((b, s, n3), BF16),
            jax.ShapeDtypeStruct((b, s, n3), F32),
            jax.ShapeDtypeStruct((b, s, n3), BF16),
            jax.ShapeDtypeStruct((b, s // tm, tm // MOBA_BLOCK, BRANCH_WIDTH), F32),
        ],
        compiler_params=_cparams(("parallel", "parallel")),
        name="inproj",
    )(x, sc, sh, w_in)


def _branches_kernel(z_ref, halo_ref, wp_ref, ps_ref, lg_ref, lb_ref, wsp_ref, bsp_ref,
                     ob_ref, oc_ref, *, tm):
    i = pl.program_id(1)
    w = BRANCH_WIDTH
    gd = w // len(POOL_WINDOWS)
    hw = POOL_MAX_WINDOW

    zp = z_ref[0, :, 0:w]
    halo = jnp.where(i > 0, halo_ref[0, :, 0:w], 0.0)
    ext = jnp.concatenate([halo, zp], axis=0)
    pos = (i * tm + lax.broadcasted_iota(jnp.int32, (tm, 1), 0) + 1).astype(F32)
    outs = []
    for g, win in enumerate(POOL_WINDOWS):
        e = ext[:, g * gd:(g + 1) * gd]
        shift = 1
        while shift < win:
            e = e + pltpu.roll(e, shift, axis=0)
            shift *= 2
        cnt = jnp.minimum(pos, float(win))
        pooled = e[hw:, :] / cnt - zp[:, g * gd:(g + 1) * gd]
        outs.append(jnp.dot(pooled.astype(BF16), wp_ref[g].astype(BF16), preferred_element_type=F32))
    ob_ref[0] = (jnp.concatenate(outs, axis=1) * ps_ref[...]).astype(BF16)

    u = jax.nn.gelu(z_ref[0, :, w:2 * w], approximate=True)
    v = _layer_norm(jax.nn.gelu(z_ref[0, :, 2 * w:3 * w], approximate=True), lg_ref[...], lb_ref[...])
    r = lax.broadcasted_iota(jnp.int32, (SGU_CHUNK, SGU_CHUNK), 0)
    c = lax.broadcasted_iota(jnp.int32, (SGU_CHUNK, SGU_CHUNK), 1)
    sg = w // SGU_GROUPS
    for g in range(SGU_GROUPS):
        wc = jnp.where(r >= c, wsp_ref[g], 0.0).astype(BF16)
        for ch in range(tm // SGU_CHUNK):
            rows = slice(ch * SGU_CHUNK, (ch + 1) * SGU_CHUNK)
            cols = slice(g * sg, (g + 1) * sg)
            vm = jnp.dot(wc, v[rows, cols].astype(BF16), preferred_element_type=F32) + bsp_ref[g]
            oc_ref[0, rows, cols] = (u[rows, cols] * vm).astype(BF16)


def _branches(zpuv, w_pool, pool_scale, ln_g, ln_b, w_sp, b_sp):
    b, s, n3 = zpuv.shape
    tm = min(512, s)
    w = BRANCH_WIDTH
    hw = POOL_MAX_WINDOW
    bsp = jnp.broadcast_to(b_sp[:, :, None], (SGU_GROUPS, SGU_CHUNK, w // SGU_GROUPS))
    full = lambda a: pl.BlockSpec(a.shape, lambda bi, i: (0,) * a.ndim)
    ps, lg, lb = pool_scale.reshape(1, w), ln_g.reshape(1, w), ln_b.reshape(1, w)
    return pl.pallas_call(
        functools.partial(_branches_kernel, tm=tm),
        grid=(b, s // tm),
        in_specs=[
            pl.BlockSpec((1, tm, n3), lambda bi, i: (bi, i, 0)),
            pl.BlockSpec((1, hw, n3), lambda bi, i: (bi, jnp.maximum(i * (tm // hw) - 1, 0), 0)),
            full(w_pool), full(ps), full(lg), full(lb), full(w_sp), full(bsp),
        ],
        out_specs=[pl.BlockSpec((1, tm, w), lambda bi, i: (bi, i, 0))] * 2,
        out_shape=[jax.ShapeDtypeStruct((b, s, w), BF16)] * 2,
        compiler_params=_cparams(("parallel", "parallel")),
        name="pool_sgu",
    )(zpuv, zpuv, w_pool, ps, lg, lb, w_sp, bsp)


_NT = (((1,), (1,)), ((), ()))


def _visible_keys(tq, tk, rel_tile, c0):
    return min(max(c0 % tq - rel_tile * tk + ROW_CHUNK, 0), tk)


def _flash_tile(qa, ka_next, vb, s_ref, s_next, m_sc, acc_sc, patch=None, keys=None, keys_next=None):
    n_rows, tk = s_next.shape
    if vb is not None:
        lane = lax.broadcasted_iota(jnp.int32, vb.shape, 1)
        va = jnp.concatenate([vb, jnp.where(lane == 0, 1.0, 0.0).astype(BF16)], axis=1)
    for c0 in range(0, n_rows, ROW_CHUNK):
        rows = slice(c0, c0 + ROW_CHUNK)
        nk = tk if keys is None else keys(c0)
        if vb is not None and nk > 0:
            if patch is not None:
                patch(c0)
            s = s_ref[rows, :nk]
            m_old = m_sc[rows]
            m_new = jnp.maximum(m_old, jnp.max(s, axis=-1, keepdims=True))
            p = jnp.exp(s - m_new).astype(BF16)
            acc_sc[rows] = (jnp.exp(m_old - m_new) * acc_sc[rows]
                            + jnp.dot(p, va[:nk], preferred_element_type=F32))
            m_sc[rows] = m_new
        nk = tk if keys_next is None else keys_next(c0)
        if ka_next is not None and nk > 0:
            s_next[rows, :nk] = lax.dot_general(qa[rows], ka_next[:nk], _NT, preferred_element_type=F32)


def _near_patches(s_ref, bias_diag, bias_prev, tq, tk, rel_tile, c0):
    t = BIAS_TILE
    hh = c0 // tq
    hi = _visible_keys(tq, tk, rel_tile, c0)
    for r0 in range(c0 % tq, c0 % tq + ROW_CHUNK, t):
        rows = slice(hh * tq + r0, hh * tq + r0 + t)
        rel = r0 - rel_tile * tk
        if rel == tk:
            s_ref[rows, tk - t:tk] = s_ref[rows, tk - t:tk] + bias_prev(hh)
        if not 0 <= rel < tk:
            continue
        s_ref[rows, rel:rel + t] = s_ref[rows, rel:rel + t] + bias_diag(hh)
        if rel >= t:
            s_ref[rows, rel - t:rel] = s_ref[rows, rel - t:rel] + bias_prev(hh)
        if rel + t < hi:
            s_ref[rows, rel + t:hi] = jnp.full((t, hi - rel - t), MASK_VALUE, F32)


def _run_causal_tiles(i, tq, qa, keys_of, values_of, s2_ref, m_sc, acc_sc, bias_diag, bias_prev):
    tk = s2_ref.shape[-1]
    ratio = tq // tk
    halves = (s2_ref.at[0], s2_ref.at[s2_ref.shape[0] - 1])
    tile = functools.partial(_flash_tile, qa, m_sc=m_sc, acc_sc=acc_sc)
    first = ratio * i
    n_far = jnp.maximum(first - 1, 0)

    near = list(range(ratio - 1, -2, -1))
    vis = lambda rel: functools.partial(_visible_keys, tq, tk, rel)
    fix = lambda ref, rel: functools.partial(_near_patches, ref, bias_diag, bias_prev, tq, tk, rel)
    tile(keys_of(first + near[0]), None, None, halves[0], keys_next=vis(near[0]))
    for n, rel in enumerate(near):
        cur, nxt = halves[n % 2], halves[(n + 1) % 2]
        if rel >= 0:
            nxt_rel = near[n + 1]
            tile(keys_of(jnp.maximum(first + nxt_rel, 0)), values_of(first + rel), cur, nxt,
                 keys=vis(rel), keys_next=vis(nxt_rel), patch=fix(cur, rel))
        else:
            pl.when(i >= 1)(lambda cur=cur, nxt=nxt, rel=rel: tile(
                keys_of(0), values_of(first + rel), cur, nxt, patch=fix(cur, rel)))
    a, b = halves[len(near) % 2], halves[(len(near) + 1) % 2]

    def far_pair(k, carry):
        tile(keys_of(2 * k + 1), values_of(2 * k), a, b)
        tile(keys_of(2 * k + 2), values_of(2 * k + 1), b, a)
        return carry

    n_pairs = jnp.maximum(n_far - 1, 0) // 2
    lax.fori_loop(0, n_pairs, far_pair, 0)
    first_left = 2 * n_pairs

    @pl.when(n_far - first_left == 2)
    def _():
        tile(keys_of(first_left + 1), values_of(first_left), a, b)
        tile(None, values_of(first_left + 1), b, b)

    @pl.when(n_far - first_left == 1)
    def _():
        tile(None, values_of(first_left), a, a)


def _attn_tiles(s):
    tq = min(ATTN_Q_TILE, s)
    return tq, min(ATTN_KV_TILE, tq)


def _attn_scratch(tq, tk):
    return [pltpu.VMEM((LOGIT_BUFFERS, 2 * tq, tk), F32), pltpu.VMEM((2 * tq, 1), F32),
            pltpu.VMEM((2 * tq, 2 * LANES), F32)]


def _diff_attn_kernel(lam_ref, g_ref, q_ref, k_ref, v_ref, bd_ref, bp_ref, o_ref,
                      s_ref, m_sc, acc_sc, *, tq, tk, lam_init):
    i = pl.program_id(2)
    q = q_ref[0]
    lane = lax.broadcasted_iota(jnp.int32, (1, LANES), 1)
    zero = jnp.zeros_like(q)
    qq = jnp.concatenate([jnp.where(lane < HEAD_DIM, q, zero),
                          jnp.where(lane >= HEAD_DIM, q, zero)], axis=0)
    m_sc[...] = jnp.full(m_sc.shape, MASK_VALUE, F32)
    acc_sc[...] = jnp.zeros(acc_sc.shape, F32)

    tile = lambda ref, j: ref[0, pl.ds(pl.multiple_of(j * tk, tk), tk), :]
    _run_causal_tiles(i, tq, qq, functools.partial(tile, k_ref), functools.partial(tile, v_ref),
                      s_ref, m_sc, acc_sc, lambda hh: bd_ref[0], lambda hh: bp_ref[0])

    acc = acc_sc[...]
    o = acc[:, :LANES] / acc[:, LANES:LANES + 1]
    lam = (jnp.exp(jnp.sum(lam_ref[0:1, :] * lam_ref[1:2, :], axis=-1, keepdims=True))
           - jnp.exp(jnp.sum(lam_ref[2:3, :] * lam_ref[3:4, :], axis=-1, keepdims=True)) + lam_init)
    od = o[:tq] - lam * o[tq:]
    ms = jnp.mean(jnp.square(od), -1, keepdims=True)
    o_ref[0] = ((od * lax.rsqrt(ms + NORM_EPS) * g_ref[...]) * (1.0 - lam_init)).astype(BF16)


def _diff_attention(qkv, lams, norm_g, bias_diag, bias_prev, lam_init):
    b, s, _ = qkv.shape
    tq, tk = _attn_tiles(s)
    t = BIAS_TILE
    nh = DIFF_HEADS
    whole = lambda col: pl.BlockSpec((1, s, LANES), lambda bi, h, i: (bi, 0, col + h))
    return pl.pallas_call(
        functools.partial(_diff_attn_kernel, tq=tq, tk=tk, lam_init=lam_init),
        grid=(b, nh, s // tq),
        in_specs=[
            pl.BlockSpec(lams.shape, lambda bi, h, i: (0, 0)),
            pl.BlockSpec(norm_g.shape, lambda bi, h, i: (0, 0)),
            pl.BlockSpec((1, tq, LANES), lambda bi, h, i: (bi, i, h)),
            whole(nh), whole(2 * nh),
            pl.BlockSpec((1, t, t), lambda bi, h, i: (h, 0, 0)),
            pl.BlockSpec((1, t, t), lambda bi, h, i: (h, 0, 0)),
        ],
        out_specs=pl.BlockSpec((1, tq, LANES), lambda bi, h, i: (bi, i, h)),
        out_shape=jax.ShapeDtypeStruct((b, s, BRANCH_WIDTH), BF16),
        scratch_shapes=_attn_scratch(tq, tk),
        compiler_params=_cparams(("parallel", "parallel", "arbitrary"), ATTN_VMEM_LIMIT),
        name="diff_attention",
    )(lams, norm_g, qkv, qkv, qkv, bias_diag, bias_prev)


def _moba_kernel(q_ref, km_ref, k_ref, v_ref, bd_ref, bp_ref, o_ref,
                 qa_sc, s_ref, m_sc, acc_sc, *, tq, tk):
    i = pl.program_id(2)
    shift = int(math.log2(MOBA_BLOCK))
    q = q_ref[0]
    lane = lax.broadcasted_iota(jnp.int32, (1, LANES), 1)
    row = lax.broadcasted_iota(jnp.int32, (tk, 1), 0)
    n_slot = -(-(k_ref.shape[1] // MOBA_BLOCK) // 8) * 8
    blk = lax.broadcasted_iota(jnp.int32, (n_slot, tq), 0).astype(F32)
    own = lax.shift_right_logical(i * tq + lax.broadcasted_iota(jnp.int32, (1, tq), 1),
                                  shift).astype(F32)
    zero = jnp.zeros_like(q)
    km = km_ref[0, :n_slot, :].astype(BF16)
    ninf = jnp.float32(-jnp.inf)
    for hh in range(2):
        qh = jnp.where((lane >= hh * HEAD_DIM) & (lane < (hh + 1) * HEAD_DIM), q, zero)
        gate = lax.dot_general(km, qh, _NT, preferred_element_type=F32)
        g = jnp.where(blk < own, gate, ninf)
        sel = blk >= own
        for _ in range(MOBA_TOPK):
            mx = jnp.max(g, axis=0, keepdims=True)
            first = jnp.min(jnp.where(g == mx, blk, float(LANES)), axis=0, keepdims=True)
            hit = blk == first
            sel = sel | (hit & (mx > ninf))
            g = jnp.where(hit, ninf, g)
        bias = jnp.concatenate([jnp.where(sel, 0.0, MASK_VALUE), jnp.zeros((LANES - n_slot, tq), F32)], axis=0)
        qa_sc[hh * tq:(hh + 1) * tq, :] = jnp.concatenate([qh, bias.T.astype(BF16)], axis=1)
    m_sc[...] = jnp.full(m_sc.shape, MASK_VALUE, F32)
    acc_sc[...] = jnp.zeros(acc_sc.shape, F32)

    def keys_of(j):
        start = pl.multiple_of(j * tk, tk)
        kblk = lax.shift_right_logical(start + row, shift)
        return jnp.concatenate([k_ref[0, pl.ds(start, tk), :],
                                jnp.where(lane == kblk, 1.0, 0.0).astype(BF16)], axis=1)

    values_of = lambda j: v_ref[0, pl.ds(pl.multiple_of(j * tk, tk), tk), :]
    _run_causal_tiles(i, tq, qa_sc, keys_of, values_of, s_ref, m_sc, acc_sc,
                      lambda hh: bd_ref[0, hh], lambda hh: bp_ref[0, hh])

    acc = acc_sc[...]
    o = acc[:, :LANES] / acc[:, LANES:LANES + 1]
    o_ref[0] = jnp.where(lane < HEAD_DIM, o[:tq], o[tq:]).astype(BF16)


def _moba_attention(qkv, kmean, bias_diag, bias_prev):
    b, s, _ = qkv.shape
    tq, tk = _attn_tiles(s)
    t = BIAS_TILE
    npair = MOBA_HEADS // 2
    whole = lambda col: pl.BlockSpec((1, s, LANES), lambda bi, p, i: (bi, 0, col + p))
    return pl.pallas_call(
        functools.partial(_moba_kernel, tq=tq, tk=tk),
        grid=(b, npair, s // tq),
        in_specs=[
            pl.BlockSpec((1, tq, LANES), lambda bi, p, i: (bi, i, p)),
            pl.BlockSpec((1, LANES, LANES), lambda bi, p, i: (bi, 0, p)),
            whole(npair), whole(2 * npair),
            pl.BlockSpec((1, 2, t, t), lambda bi, p, i: (p, 0, 0, 0)),
            pl.BlockSpec((1, 2, t, t), lambda bi, p, i: (p, 0, 0, 0)),
        ],
        out_specs=pl.BlockSpec((1, tq, LANES), lambda bi, p, i: (bi, i, p)),
        out_shape=jax.ShapeDtypeStruct((b, s, BRANCH_WIDTH), BF16),
        scratch_shapes=[pltpu.VMEM((2 * tq, 2 * LANES), BF16)] + _attn_scratch(tq, tk),
        compiler_params=_cparams(("parallel", "parallel", "arbitrary"), ATTN_VMEM_LIMIT),
        name="moba_attention",
    )(qkv, kmean, qkv, qkv, bias_diag, bias_prev)


def _merge_kernel(h_ref, oa_ref, ob_ref, oc_ref, od_ref, wg_ref, bg_ref, wb_ref, out_ref):
    h = h_ref[...]
    acc = None
    for br, o_ref in enumerate((oa_ref, ob_ref, oc_ref, od_ref)):
        gate = _sigmoid(jnp.dot(h, wg_ref[br], preferred_element_type=F32) + bg_ref[br])
        term = gate * jnp.dot(o_ref[...], wb_ref[br], preferred_element_type=F32)
        acc = term if acc is None else acc + term
    out_ref[...] = acc.astype(BF16)


def _merge(h, branches, w_gate, b_gate, w_branch):
    t, d = h.shape
    w = BRANCH_WIDTH
    tm = min(512, t)
    tn = 512
    nbr = len(branches)
    row = lambda j, i: (i, 0)
    return pl.pallas_call(
        _merge_kernel,
        grid=(d // tn, t // tm),
        in_specs=[pl.BlockSpec((tm, d), row)] + [pl.BlockSpec((tm, w), row)] * nbr + [
            pl.BlockSpec((nbr, d, tn), lambda j, i: (0, 0, j)),
            pl.BlockSpec((nbr, 1, tn), lambda j, i: (0, 0, j)),
            pl.BlockSpec((nbr, w, tn), lambda j, i: (0, 0, j)),
        ],
        out_specs=pl.BlockSpec((tm, tn), lambda j, i: (i, j)),
        out_shape=jax.ShapeDtypeStruct((t, d), BF16),
        compiler_params=_cparams(("parallel", "parallel")),
        name="gated_merge",
    )(h, *branches, w_gate, b_gate, w_branch)


def _route(aff, sel):
    ng, pg = N_EXPERT_GROUPS, EXPERTS_PER_GROUP
    srow = [sel[e:e + 1] for e in range(N_EXPERTS)]
    arow = [aff[e:e + 1] for e in range(N_EXPERTS)]
    best_score, best_grp = None, None
    for g in range(ng):
        rows = srow[g * pg:(g + 1) * pg]
        score = None
        for a in range(pg):
            for b in range(a + 1, pg):
                pair = rows[a] + rows[b]
                score = pair if score is None else jnp.maximum(score, pair)
        if best_score is None:
            best_score, best_grp = score, jnp.zeros_like(score, dtype=jnp.int32)
        else:
            better = score > best_score
            best_score = jnp.where(better, score, best_score)
            best_grp = jnp.where(better, g, best_grp)

    def pick(rows_by_group):
        out = rows_by_group[0]
        for g in range(1, ng):
            out = jnp.where(best_grp == g, rows_by_group[g], out)
        return out

    cand_s = [pick([srow[g * pg + k] for g in range(ng)]) for k in range(pg)]
    cand_a = [pick([arow[g * pg + k] for g in range(ng)]) for k in range(pg)]

    def argmax_first(vals, exclude):
        bv, bi, ba = None, None, None
        for k in range(pg):
            v = vals[k] if exclude is None else jnp.where(exclude == k, -jnp.inf, vals[k])
            if bv is None:
                bv, bi, ba = v, jnp.zeros_like(best_grp), cand_a[k]
            else:
                better = v > bv
                bv = jnp.where(better, v, bv)
                bi = jnp.where(better, k, bi)
                ba = jnp.where(better, cand_a[k], ba)
        return bi, ba

    i0, a0 = argmax_first(cand_s, None)
    i1, a1 = argmax_first(cand_s, i0)
    tot = a0 + a1
    return best_grp * pg + i0, best_grp * pg + i1, a0 / tot, a1 / tot


def _outproj_kernel(m_ref, x_ref, wo_ref, g1_ref, lg_ref, lb_ref, sc2_ref, sh2_ref, wr_ref, br_ref,
                    x1_ref, h2_ref, ri_ref, rw_ref, cnt_ref, *, tm):
    @pl.when((pl.program_id(0) == 0) & (pl.program_id(1) == 0))
    def _():
        cnt_ref[...] = jnp.zeros(cnt_ref.shape, F32)

    rc = min(OUTPROJ_ROW_CHUNK, tm)
    erow = lax.broadcasted_iota(jnp.int32, (N_EXPERTS, rc), 0)
    earlier = (lax.broadcasted_iota(jnp.int32, (rc, rc), 0)
               < lax.broadcasted_iota(jnp.int32, (rc, rc), 1)).astype(BF16)
    for r0 in range(0, tm, rc):
        rows = slice(r0, r0 + rc)
        y = jnp.dot(m_ref[0, rows], wo_ref[...], preferred_element_type=F32)
        x1 = _layer_norm(ALPHA * x_ref[0, rows] + (1.0 + g1_ref[0]) * y, lg_ref[...], lb_ref[...])
        x1_ref[0, rows] = x1
        h2 = x1 * (1.0 + sc2_ref[0]) + sh2_ref[0]
        h2_ref[0, rows] = h2
        logits = lax.dot_general(wr_ref[...], h2.astype(BF16), _NT, preferred_element_type=F32)
        aff = _sigmoid(logits)
        e0, e1, w0, w1 = _route(aff, aff + br_ref[...])

        hit0, hit1 = erow == e0, erow == e1
        hits = jnp.where(hit0 | hit1, 1.0, 0.0)
        before = jnp.dot(hits.astype(BF16), earlier, preferred_element_type=F32) + cnt_ref[:, 0:1]
        rank = lambda hit: jnp.sum(jnp.where(hit, before, 0.0), axis=0, keepdims=True).astype(jnp.int32)
        cnt_ref[...] = cnt_ref[...] + jnp.sum(hits, axis=1, keepdims=True)
        ri_ref[0, :, rows] = jnp.concatenate(
            [e0, e1, rank(hit0), rank(hit1), jnp.zeros((4, rc), jnp.int32)], axis=0)
        rw_ref[0, :, rows] = jnp.concatenate([w0, w1, jnp.zeros((6, rc), F32)], axis=0)


def _outproj(merged, x, w_out, g1, ln_g, ln_b, sc2, sh2, w_router_t, b_router):
    b, s, d = x.shape
    tm = min(512, s)
    row = lambda bi, i: (bi, i, 0)
    vec = lambda bi, i: (bi, 0, 0)
    full = lambda a: pl.BlockSpec(a.shape, lambda bi, i: (0,) * a.ndim)
    lg, lb = ln_g.reshape(1, d), ln_b.reshape(1, d)
    br = b_router.reshape(N_EXPERTS, 1)
    return pl.pallas_call(
        functools.partial(_outproj_kernel, tm=tm),
        grid=(b, s // tm),
        in_specs=[
            pl.BlockSpec((1, tm, d), row), pl.BlockSpec((1, tm, d), row), full(w_out),
            pl.BlockSpec((1, 1, d), vec), full(lg), full(lb),
            pl.BlockSpec((1, 1, d), vec), pl.BlockSpec((1, 1, d), vec),
            full(w_router_t), full(br),
        ],
        out_specs=[
            pl.BlockSpec((1, tm, d), row), pl.BlockSpec((1, tm, d), row),
            pl.BlockSpec((1, 8, tm), lambda bi, i: (bi, 0, i)),
            pl.BlockSpec((1, 8, tm), lambda bi, i: (bi, 0, i)),
            pl.BlockSpec((N_EXPERTS, LANES), lambda bi, i: (0, 0)),
        ],
        out_shape=[
            jax.ShapeDtypeStruct((b, s, d), F32), jax.ShapeDtypeStruct((b, s, d), F32),
            jax.ShapeDtypeStruct((b, 8, s), jnp.int32), jax.ShapeDtypeStruct((b, 8, s), F32),
            jax.ShapeDtypeStruct((N_EXPERTS, LANES), F32),
        ],
        compiler_params=_cparams(("arbitrary", "arbitrary")),
        name="outproj_ln_router",
    )(merged.reshape(b, s, d), x, w_out, g1, lg, lb, sc2, sh2, w_router_t, br)


def _row_copy(src, src_row, dst, dst_row, sem):
    return pltpu.make_async_copy(src.at[pl.ds(src_row, 1)], dst.at[pl.ds(dst_row, 1)], sem)


def _dispatch_kernel(tail_ref, nu_ref, dest_ref, h2_ref, xs_ref, zeros_sc, sem, zsem, *, rows):
    @pl.when(pl.program_id(0) == 0)
    def _():
        zeros_sc[...] = jnp.zeros(zeros_sc.shape, F32)
        fill = lambda row0: pltpu.make_async_copy(
            zeros_sc, xs_ref.at[pl.ds(pl.multiple_of(row0, SLOT_BLOCK), SLOT_BLOCK)], zsem)
        unused = (nu_ref[0], xs_ref.shape[0] // SLOT_BLOCK)
        for e in range(N_EXPERTS):
            pl.when(tail_ref[e] >= 0)(lambda e=e: fill(tail_ref[e]).start())
        lax.fori_loop(*unused, lambda b, c: (fill(b * SLOT_BLOCK).start(), c)[1], 0)
        for e in range(N_EXPERTS):
            pl.when(tail_ref[e] >= 0)(lambda e=e: fill(tail_ref[e]).wait())
        lax.fori_loop(*unused, lambda b, c: (fill(b * SLOT_BLOCK).wait(), c)[1], 0)

    def issue(r, carry):
        for k in range(TOP_K):
            _row_copy(h2_ref, r, xs_ref, dest_ref[0, 0, TOP_K * r + k], sem).start()
        return carry

    lax.fori_loop(0, rows, issue, 0, unroll=4)
    for k in range(TOP_K):
        pltpu.make_async_copy(h2_ref, xs_ref.at[pl.ds(0, rows)], sem).wait()


def _dispatch(dest, tail_start, n_used, h2, n_slots):
    t, d = h2.shape
    rows = min(512, t)
    return pl.pallas_call(
        functools.partial(_dispatch_kernel, rows=rows),
        grid_spec=pltpu.PrefetchScalarGridSpec(
            num_scalar_prefetch=2,
            grid=(t // rows,),
            in_specs=[
                pl.BlockSpec((1, 1, TOP_K * rows), lambda i, tail, nu: (i, 0, 0), memory_space=pltpu.SMEM),
                pl.BlockSpec((rows, d), lambda i, tail, nu: (i, 0)),
            ],
            out_specs=pl.BlockSpec(memory_space=pl.ANY),
            scratch_shapes=[pltpu.VMEM((SLOT_BLOCK, d), F32), pltpu.SemaphoreType.DMA(()),
                            pltpu.SemaphoreType.DMA(())],
        ),
        out_shape=jax.ShapeDtypeStruct((n_slots, d), F32),
        compiler_params=_cparams(("arbitrary",)),
        name="moe_dispatch",
    )(tail_start, n_used, dest.reshape(t // rows, 1, TOP_K * rows), h2)


def _cast_kernel(w_ref, o_ref):
    o_ref[0] = w_ref[0, 0].astype(BF16)


def _cast_layer(w, layer, rows):
    _, e, r, c = w.shape
    return pl.pallas_call(
        _cast_kernel,
        grid=(e, r // rows),
        in_specs=[pl.BlockSpec((1, 1, rows, c), lambda ei, i: (layer, ei, i, 0))],
        out_specs=pl.BlockSpec((1, rows, c), lambda ei, i: (ei, i, 0)),
        out_shape=jax.ShapeDtypeStruct((e, r, c), BF16),
        compiler_params=_cparams(("parallel", "parallel")),
        name="expert_weight_cast",
    )(w)


def _expert_kernel(be_ref, nu_ref, xs_ref, wg_ref, wu_ref, wd_ref, ys_ref):
    del be_ref
    used = pl.program_id(0) < nu_ref[0]

    @pl.when(used)
    def _():
        xb = xs_ref[...].astype(BF16)
        g = jnp.dot(xb, wg_ref[0], preferred_element_type=F32)
        u = jnp.dot(xb, wu_ref[0], preferred_element_type=F32)
        act = ((g * _sigmoid(g)) * u).astype(BF16)
        ys_ref[...] = jnp.dot(act, wd_ref[0], preferred_element_type=F32)

    @pl.when(jnp.logical_not(used))
    def _():
        ys_ref[...] = jnp.zeros(ys_ref.shape, F32)


def _expert_ffn(blk_expert, n_used, xs, w_g, w_u, w_d):
    n_slots, d = xs.shape
    f = w_d.shape[1]
    blk = SLOT_BLOCK
    last = lambda b, nu: jnp.minimum(b, nu[0] - 1)
    wspec = lambda shape: pl.BlockSpec(shape, lambda b, be, nu: (be[last(b, nu)], 0, 0),
                                       pipeline_mode=pl.Buffered(1))
    return pl.pallas_call(
        _expert_kernel,
        grid_spec=pltpu.PrefetchScalarGridSpec(
            num_scalar_prefetch=2,
            grid=(n_slots // blk,),
            in_specs=[
                pl.BlockSpec((blk, d), lambda b, be, nu: (last(b, nu), 0)),
                wspec((1, d, f)), wspec((1, d, f)), wspec((1, f, d)),
            ],
            out_specs=pl.BlockSpec((blk, d), lambda b, be, nu: (b, 0)),
        ),
        out_shape=jax.ShapeDtypeStruct((n_slots, d), F32),
        compiler_params=_cparams(("arbitrary",)),
        name="moe_experts",
    )(blk_expert, n_used, xs, w_g, w_u, w_d)


def _combine_kernel(dest_ref, dest_next_ref, w_ref, x1_ref, g2_ref, lg_ref, lb_ref, ys_ref, o_ref,
                    buf, sem, *, rows):
    step = pl.program_id(0) * pl.num_programs(1) + pl.program_id(1)
    n_steps = pl.num_programs(0) * pl.num_programs(1)
    slot = step % 2

    def gather(dref, sl):
        def issue(r, carry):
            for k in range(TOP_K):
                _row_copy(ys_ref, dref[0, 0, TOP_K * r + k], buf.at[sl, k], r, sem.at[sl]).start()
            return carry
        lax.fori_loop(0, rows, issue, 0, unroll=4)

    pl.when(step == 0)(lambda: gather(dest_ref, 0))
    pl.when(step + 1 < n_steps)(lambda: gather(dest_next_ref, 1 - slot))
    for k in range(TOP_K):
        pltpu.make_async_copy(ys_ref.at[pl.ds(0, rows)], buf.at[slot, k], sem.at[slot]).wait()
    y = buf[slot, 0] * w_ref[:, 0:1] + buf[slot, 1] * w_ref[:, 1:2]
    o_ref[0] = _layer_norm(ALPHA * x1_ref[0] + (1.0 + g2_ref[0]) * y, lg_ref[...], lb_ref[...])


def _combine(dest, wts, x1, g2, ln_g, ln_b, ys):
    b, s, d = x1.shape
    rows = min(256, s)
    nt = s // rows
    lg, lb = ln_g.reshape(1, d), ln_b.reshape(1, d)
    full = lambda a: pl.BlockSpec(a.shape, lambda bi, i: (0,) * a.ndim)
    dest3 = dest.reshape(b * nt, 1, TOP_K * rows)
    smem = lambda ahead: pl.BlockSpec(
        (1, 1, TOP_K * rows), lambda bi, i: (jnp.minimum(bi * nt + i + ahead, b * nt - 1), 0, 0),
        memory_space=pltpu.SMEM)
    return pl.pallas_call(
        functools.partial(_combine_kernel, rows=rows),
        grid=(b, nt),
        in_specs=[
            smem(0), smem(1),
            pl.BlockSpec((rows, TOP_K), lambda bi, i: (bi * nt + i, 0)),
            pl.BlockSpec((1, rows, d), lambda bi, i: (bi, i, 0)),
            pl.BlockSpec((1, 1, d), lambda bi, i: (bi, 0, 0)),
            full(lg), full(lb),
            pl.BlockSpec(memory_space=pl.ANY),
        ],
        out_specs=pl.BlockSpec((1, rows, d), lambda bi, i: (bi, i, 0)),
        out_shape=jax.ShapeDtypeStruct((b, s, d), F32),
        scratch_shapes=[pltpu.VMEM((2, TOP_K, rows, d), F32), pltpu.SemaphoreType.DMA((2,))],
        compiler_params=_cparams(("arbitrary", "arbitrary")),
        name="moe_combine_ln",
    )(dest3, dest3, wts, x1, g2, lg, lb, ys)


def _slot_plan(e_flat, rank, counts, n_slots):
    blk = SLOT_BLOCK
    padded = (counts + blk - 1) // blk * blk
    pend = jnp.cumsum(padded)
    experts = jnp.arange(N_EXPERTS, dtype=jnp.int32)
    dest = jnp.sum(jnp.where(e_flat[:, None] == experts[None, :], (pend - padded)[None, :], 0), axis=1) + rank
    nblk = n_slots // blk
    blk_start = jnp.arange(nblk, dtype=jnp.int32) * blk
    blk_expert = jnp.minimum(jnp.sum(pend[None, :] <= blk_start[:, None], axis=1), N_EXPERTS - 1)
    tail_start = jnp.where(padded > 0, pend - blk, -1)
    return (dest.astype(jnp.int32), blk_expert.astype(jnp.int32), (pend[-1:] // blk).astype(jnp.int32),
            tail_start.astype(jnp.int32))


def _t5_bucket(dist):
    max_exact = REL_BUCKETS // 2
    d = jnp.maximum(dist, 1).astype(F32)
    large = max_exact + (jnp.log(d / max_exact) / math.log(REL_MAX_DIST / max_exact)
                         * (REL_BUCKETS - max_exact)).astype(jnp.int32)
    large = jnp.minimum(large, REL_BUCKETS - 1)
    return jnp.where(dist < max_exact, dist, large)


def _bias_tiles(rel_table, t):
    r = jnp.arange(t, dtype=jnp.int32)[:, None]
    c = jnp.arange(t, dtype=jnp.int32)[None, :]
    dd = r - c

    def lookup(dist):
        onehot = (_t5_bucket(dist)[:, :, None] == jnp.arange(REL_BUCKETS, dtype=jnp.int32)).astype(F32)
        return jnp.einsum('rck,kh->hrc', onehot, rel_table, precision=lax.Precision.HIGHEST)

    diag = jnp.where(dd >= 0, lookup(jnp.maximum(dd, 0)), MASK_VALUE)
    return diag, lookup(dd + t)


def kernel(x, c, rel_bias, w_router, b_router, w_ada, b_ada, w_in, lam_q1, lam_k1, lam_q2, lam_k2,
           diff_norm_g, w_pool, pool_scale, sgu_ln_g, sgu_ln_b, w_spatial, b_spatial, w_branch,
           w_gate, b_gate, w_out, ln1_g, ln1_b, w_e_gate, w_e_up, w_e_down, ln2_g, ln2_b):
    b, s, d = x.shape
    t = b * s
    assert d == D_MODEL and s % MOBA_BLOCK == 0 and s // MOBA_BLOCK <= LANES
    assert int(16 * 8 ** (15 / 16)) + 1 < MOBA_BLOCK and REL_BUCKETS == 32 and REL_MAX_DIST == 128

    assert s % min(ATTN_Q_TILE, s) == 0 and BIAS_TILE == MOBA_BLOCK
    diag, prev = _bias_tiles(rel_bias - rel_bias[REL_BUCKETS - 1][None, :], BIAS_TILE)
    a_diag, a_prev = diag[:DIFF_HEADS], prev[:DIFF_HEADS]
    pair = lambda a: a[DIFF_HEADS:].reshape(MOBA_HEADS // 2, 2, BIAS_TILE, BIAS_TILE)
    d_diag, d_prev = pair(diag), pair(prev)

    mod = _ada_mod(c, w_ada, b_ada)
    w_router_t = w_router.T.astype(BF16)
    n_slots = (-(-t * TOP_K // SLOT_BLOCK) + N_EXPERTS) * SLOT_BLOCK

    for layer in range(DEPTH):
        sh1, sc1, g1, sh2, sc2, g2 = (m[:, None, :] for m in jnp.split(mod[layer], 6, axis=-1))
        lam_init = 0.8 - 0.6 * math.exp(-0.3 * layer)
        lams = jnp.stack([lam_q1[layer], lam_k1[layer], lam_q2[layer], lam_k2[layer]])

        h, qkv_a, zpuv, qkv_d, kmean = _inproj(x, sc1, sh1, w_in[layer].astype(BF16))
        o_b, o_c = _branches(zpuv, w_pool[layer], pool_scale[layer], sgu_ln_g[layer], sgu_ln_b[layer],
                             w_spatial[layer], b_spatial[layer])
        o_a = _diff_attention(qkv_a, lams, diff_norm_g[layer].reshape(1, 2 * HEAD_DIM),
                              a_diag, a_prev, lam_init)
        kmean = kmean.reshape(b, s // MOBA_BLOCK, BRANCH_WIDTH)
        kmean = jnp.pad(kmean, ((0, 0), (0, LANES - s // MOBA_BLOCK), (0, 0)))
        o_d = _moba_attention(qkv_d, kmean, d_diag, d_prev)

        flat = lambda a: a.reshape(t, a.shape[-1])
        merged = _merge(flat(h), [flat(o_a), flat(o_b), flat(o_c), flat(o_d)],
                        w_gate[layer].astype(BF16), b_gate[layer][:, None, :], w_branch[layer].astype(BF16))
        x1, h2, ridx, rwts, counts = _outproj(
            merged, x, w_out[layer].astype(BF16), g1, ln1_g[layer], ln1_b[layer], sc2, sh2, w_router_t, b_router)

        per_assignment = lambda rows: jnp.transpose(rows, (0, 2, 1)).reshape(t * TOP_K)
        wts = per_assignment(rwts[:, :TOP_K, :]).reshape(t, TOP_K)
        dest, blk_expert, n_used, tail_start = _slot_plan(
            per_assignment(ridx[:, :TOP_K, :]), per_assignment(ridx[:, TOP_K:2 * TOP_K, :]),
            counts[:, 0].astype(jnp.int32), n_slots)
        xs = _dispatch(dest, tail_start, n_used, h2.reshape(t, d), n_slots)
        ys = _expert_ffn(blk_expert, n_used, xs, _cast_layer(w_e_gate, layer, 512),
                         _cast_layer(w_e_up, layer, 512), _cast_layer(w_e_down, layer, EXPERT_FF // 4))
        x = _combine(dest, wts, x1, g2, ln2_g[layer], ln2_b[layer], ys)
    return x
```

```python
import functools
import math

import jax
import jax.numpy as jnp
from jax import lax
from jax.experimental import pallas as pl
from jax.experimental.pallas import tpu as pltpu

F32 = jnp.float32
BF16 = jnp.bfloat16

D_MODEL = 2048
DEPTH = 2
BRANCH_WIDTH = 512
HEAD_DIM = 64
N_SECTIONS = 9
DIFF_HEADS = 4
POOL_WINDOWS = (2, 4, 8, 16)
POOL_MAX_WINDOW = 16
SGU_CHUNK = 128
SGU_GROUPS = 4
MOBA_HEADS = 8
MOBA_BLOCK = 256
MOBA_TOPK = 3
REL_BUCKETS = 32
REL_MAX_DIST = 128
N_EXPERTS = 16
N_EXPERT_GROUPS = 4
EXPERTS_PER_GROUP = 4
TOP_K = 2
EXPERT_FF = 1408
ALPHA = (2 * DEPTH) ** 0.25
NORM_EPS = 1e-5

LANES = 128
MASK_VALUE = -1e30
ATTN_Q_TILE = 1024
ATTN_KV_TILE = 1024
LOGIT_BUFFERS = 2
BIAS_TILE = 256
ROW_CHUNK = 512
NEAR_ROW_CHUNK = 256
SLOT_BLOCK = 512
OUTPROJ_ROW_CHUNK = 512
VMEM_LIMIT = 56 * 2 ** 20
ATTN_VMEM_LIMIT = 62 * 2 ** 20


def _cparams(semantics, vmem=VMEM_LIMIT):
    return pltpu.CompilerParams(dimension_semantics=semantics, vmem_limit_bytes=vmem)


def _sigmoid(x):
    return 1.0 / (1.0 + jnp.exp(-x))


def _layer_norm(r, g, b):
    mu = jnp.mean(r, -1, keepdims=True)
    var = jnp.mean(jnp.square(r - mu), -1, keepdims=True)
    return (r - mu) * lax.rsqrt(var + NORM_EPS) * g + b


def _ada_kernel(ct_ref, w_ref, b_ref, o_ref, *, batch):
    ct = ct_ref[...]
    ca = ct * _sigmoid(ct)
    w = w_ref[0]
    rows = [jnp.sum(w * ca[:, b:b + 1], axis=0, keepdims=True) for b in range(batch)]
    o_ref[0] = jnp.concatenate(rows, axis=0) + b_ref[0]


def _ada_mod(c, w_ada, b_ada):
    batch, d = c.shape
    depth, _, n = w_ada.shape
    tn = 512
    return pl.pallas_call(
        functools.partial(_ada_kernel, batch=batch),
        grid=(depth, n // tn),
        in_specs=[
            pl.BlockSpec((d, batch), lambda l, j: (0, 0)),
            pl.BlockSpec((1, d, tn), lambda l, j: (l, 0, j)),
            pl.BlockSpec((1, 1, tn), lambda l, j: (l, 0, j)),
        ],
        out_specs=pl.BlockSpec((1, batch, tn), lambda l, j: (l, 0, j)),
        out_shape=jax.ShapeDtypeStruct((depth, batch, n), F32),
        compiler_params=_cparams(("parallel", "parallel")),
        name="ada_mod",
    )(c.T, w_ada, b_ada.reshape(depth, 1, n))


def _inproj_kernel(x_ref, sc_ref, sh_ref, w_ref, h_ref, a_ref, p_ref, d_ref, km_ref):
    w = BRANCH_WIDTH
    h = (x_ref[0] * (1.0 + sc_ref[0]) + sh_ref[0]).astype(BF16)
    h_ref[0] = h
    for s in range(N_SECTIONS):
        z = jnp.dot(h, w_ref[:, s * w:(s + 1) * w], preferred_element_type=F32)
        if s in (0, 6):
            z = z * (HEAD_DIM ** -0.5)
        if s < 3:
            a_ref[0, :, s * w:(s + 1) * w] = z.astype(BF16)
        elif s < 6:
            p_ref[0, :, (s - 3) * w:(s - 2) * w] = z
        else:
            zb = z.astype(BF16)
            d_ref[0, :, (s - 6) * w:(s - 5) * w] = zb
            if s == 7:
                kf = zb.astype(F32)
                for blk in range(kf.shape[0] // MOBA_BLOCK):
                    km_ref[0, 0, blk:blk + 1, :] = jnp.mean(
                        kf[blk * MOBA_BLOCK:(blk + 1) * MOBA_BLOCK], axis=0, keepdims=True)


def _inproj(x, sc, sh, w_in):
    b, s, d = x.shape
    tm = min(512, s)
    n3 = 3 * BRANCH_WIDTH
    row = lambda bi, i: (bi, i, 0)
    vec = lambda bi, i: (bi, 0, 0)
    return pl.pallas_call(
        _inproj_kernel,
        grid=(b, s // tm),
        in_specs=[
            pl.BlockSpec((1, tm, d), row),
            pl.BlockSpec((1, 1, d), vec),
            pl.BlockSpec((1, 1, d), vec),
            pl.BlockSpec(w_in.shape, lambda bi, i: (0, 0)),
        ],
        out_specs=[
            pl.BlockSpec((1, tm, d), row),
            pl.BlockSpec((1, tm, n3), row),
            pl.BlockSpec((1, tm, n3), row),
            pl.BlockSpec((1, tm, n3), row),
            pl.BlockSpec((1, 1, tm // MOBA_BLOCK, BRANCH_WIDTH), lambda bi, i: (bi, i, 0, 0)),
        ],
        out_shape=[
            jax.ShapeDtypeStruct((b, s, d), BF16),
            jax.ShapeDtypeStruct((b, s, n3), BF16),
            jax.ShapeDtypeStruct((b, s, n3), F32),
            jax.ShapeDtypeStruct((b, s, n3), BF16),
            jax.ShapeDtypeStruct((b, s // tm, tm // MOBA_BLOCK, BRANCH_WIDTH), F32),
        ],
        compiler_params=_cparams(("parallel", "parallel")),
        name="inproj",
    )(x, sc, sh, w_in)


def _branches_kernel(z_ref, halo_ref, wp_ref, ps_ref, lg_ref, lb_ref, wsp_ref, bsp_ref,
                     ob_ref, oc_ref, *, tm):
    i = pl.program_id(1)
    w = BRANCH_WIDTH
    gd = w // len(POOL_WINDOWS)
    hw = POOL_MAX_WINDOW

    zp = z_ref[0, :, 0:w]
    halo = jnp.where(i > 0, halo_ref[0, :, 0:w], 0.0)
    ext = jnp.concatenate([halo, zp], axis=0)
    pos = (i * tm + lax.broadcasted_iota(jnp.int32, (tm, 1), 0) + 1).astype(F32)
    outs = []
    for g, win in enumerate(POOL_WINDOWS):
        e = ext[:, g * gd:(g + 1) * gd]
        shift = 1
        while shift < win:
            e = e + pltpu.roll(e, shift, axis=0)
            shift *= 2
        cnt = jnp.minimum(pos, float(win))
        pooled = e[hw:, :] / cnt - zp[:, g * gd:(g + 1) * gd]
        outs.append(jnp.dot(pooled.astype(BF16), wp_ref[g].astype(BF16), preferred_element_type=F32))
    ob_ref[0] = (jnp.concatenate(outs, axis=1) * ps_ref[...]).astype(BF16)

    u = jax.nn.gelu(z_ref[0, :, w:2 * w], approximate=True)
    v = _layer_norm(jax.nn.gelu(z_ref[0, :, 2 * w:3 * w], approximate=True), lg_ref[...], lb_ref[...])
    r = lax.broadcasted_iota(jnp.int32, (SGU_CHUNK, SGU_CHUNK), 0)
    c = lax.broadcasted_iota(jnp.int32, (SGU_CHUNK, SGU_CHUNK), 1)
    sg = w // SGU_GROUPS
    for g in range(SGU_GROUPS):
        wc = jnp.where(r >= c, wsp_ref[g], 0.0).astype(BF16)
        for ch in range(tm // SGU_CHUNK):
            rows = slice(ch * SGU_CHUNK, (ch + 1) * SGU_CHUNK)
            cols = slice(g * sg, (g + 1) * sg)
            vm = jnp.dot(wc, v[rows, cols].astype(BF16), preferred_element_type=F32) + bsp_ref[g]
            oc_ref[0, rows, cols] = (u[rows, cols] * vm).astype(BF16)


def _branches(zpuv, w_pool, pool_scale, ln_g, ln_b, w_sp, b_sp):
    b, s, n3 = zpuv.shape
    tm = min(512, s)
    w = BRANCH_WIDTH
    hw = POOL_MAX_WINDOW
    bsp = jnp.broadcast_to(b_sp[:, :, None], (SGU_GROUPS, SGU_CHUNK, w // SGU_GROUPS))
    full = lambda a: pl.BlockSpec(a.shape, lambda bi, i: (0,) * a.ndim)
    ps, lg, lb = pool_scale.reshape(1, w), ln_g.reshape(1, w), ln_b.reshape(1, w)
    return pl.pallas_call(
        functools.partial(_branches_kernel, tm=tm),
        grid=(b, s // tm),
        in_specs=[
            pl.BlockSpec((1, tm, n3), lambda bi, i: (bi, i, 0)),
            pl.BlockSpec((1, hw, n3), lambda bi, i: (bi, jnp.maximum(i * (tm // hw) - 1, 0), 0)),
            full(w_pool), full(ps), full(lg), full(lb), full(w_sp), full(bsp),
        ],
        out_specs=[pl.BlockSpec((1, tm, w), lambda bi, i: (bi, i, 0))] * 2,
        out_shape=[jax.ShapeDtypeStruct((b, s, w), BF16)] * 2,
        compiler_params=_cparams(("parallel", "parallel")),
        name="pool_sgu",
    )(zpuv, zpuv, w_pool, ps, lg, lb, w_sp, bsp)


_NT = (((1,), (1,)), ((), ()))


def _visible_keys(tq, tk, rc, rel_tile, c0):
    return min(max(c0 % tq - rel_tile * tk + rc, 0), tk)


def _flash_tile(qa, ka_next, vb, s_ref, s_next, m_sc, acc_sc, patch=None, keys=None, keys_next=None,
                rc=ROW_CHUNK):
    n_rows, tk = s_next.shape
    if vb is not None:
        lane = lax.broadcasted_iota(jnp.int32, vb.shape, 1)
        va = jnp.concatenate([vb, jnp.where(lane == 0, 1.0, 0.0).astype(BF16)], axis=1)
    for c0 in range(0, n_rows, rc):
        rows = slice(c0, c0 + rc)
        nk = tk if keys is None else keys(c0)
        if vb is not None and nk > 0:
            if patch is not None:
                patch(c0)
            s = s_ref[rows, :nk]
            m_old = m_sc[rows]
            m_new = jnp.maximum(m_old, jnp.max(s, axis=-1, keepdims=True))
            p = jnp.exp(s - m_new).astype(BF16)
            acc_sc[rows] = (jnp.exp(m_old - m_new) * acc_sc[rows]
                            + jnp.dot(p, va[:nk], preferred_element_type=F32))
            m_sc[rows] = m_new
        nk = tk if keys_next is None else keys_next(c0)
        if ka_next is not None and nk > 0:
            s_next[rows, :nk] = lax.dot_general(qa[rows], ka_next[:nk], _NT, preferred_element_type=F32)


def _near_patches(s_ref, bias_diag, bias_prev, tq, tk, rc, rel_tile, c0):
    t = BIAS_TILE
    hh = c0 // tq
    hi = _visible_keys(tq, tk, rc, rel_tile, c0)
    for r0 in range(c0 % tq, c0 % tq + rc, t):
        rows = slice(hh * tq + r0, hh * tq + r0 + t)
        rel = r0 - rel_tile * tk
        if rel == tk:
            s_ref[rows, tk - t:tk] = s_ref[rows, tk - t:tk] + bias_prev(hh)
        if not 0 <= rel < tk:
            continue
        s_ref[rows, rel:rel + t] = s_ref[rows, rel:rel + t] + bias_diag(hh)
        if rel >= t:
            s_ref[rows, rel - t:rel] = s_ref[rows, rel - t:rel] + bias_prev(hh)
        if rel + t < hi:
            s_ref[rows, rel + t:hi] = jnp.full((t, hi - rel - t), MASK_VALUE, F32)


def _run_causal_tiles(i, tq, qa, keys_of, values_of, s2_ref, m_sc, acc_sc, bias_diag, bias_prev):
    tk = s2_ref.shape[-1]
    ratio = tq // tk
    halves = (s2_ref.at[0], s2_ref.at[s2_ref.shape[0] - 1])
    tile = functools.partial(_flash_tile, qa, m_sc=m_sc, acc_sc=acc_sc)
    first = ratio * i
    n_far = jnp.maximum(first - 1, 0)

    near = list(range(ratio - 1, -2, -1))
    rc = NEAR_ROW_CHUNK
    vis = lambda rel: functools.partial(_visible_keys, tq, tk, rc, rel)
    fix = lambda ref, rel: functools.partial(_near_patches, ref, bias_diag, bias_prev, tq, tk, rc, rel)
    tile(keys_of(first + near[0]), None, None, halves[0], keys_next=vis(near[0]), rc=rc)
    for n, rel in enumerate(near):
        cur, nxt = halves[n % 2], halves[(n + 1) % 2]
        if rel >= 0:
            nxt_rel = near[n + 1]
            tile(keys_of(jnp.maximum(first + nxt_rel, 0)), values_of(first + rel), cur, nxt,
                 keys=vis(rel), keys_next=vis(nxt_rel), patch=fix(cur, rel), rc=rc)
        else:
            pl.when(i >= 1)(lambda cur=cur, nxt=nxt, rel=rel: tile(
                keys_of(0), values_of(first + rel), cur, nxt, patch=fix(cur, rel), rc=rc))
    a, b = halves[len(near) % 2], halves[(len(near) + 1) % 2]

    def far_pair(k, carry):
        tile(keys_of(2 * k + 1), values_of(2 * k), a, b)
        tile(keys_of(2 * k + 2), values_of(2 * k + 1), b, a)
        return carry

    n_pairs = jnp.maximum(n_far - 1, 0) // 2
    lax.fori_loop(0, n_pairs, far_pair, 0)
    first_left = 2 * n_pairs

    @pl.when(n_far - first_left == 2)
    def _():
        tile(keys_of(first_left + 1), values_of(first_left), a, b)
        tile(None, values_of(first_left + 1), b, b)

    @pl.when(n_far - first_left == 1)
    def _():
        tile(None, values_of(first_left), a, a)


def _attn_tiles(s):
    tq = min(ATTN_Q_TILE, s)
    return tq, min(ATTN_KV_TILE, tq)


def _attn_scratch(tq, tk):
    return [pltpu.VMEM((LOGIT_BUFFERS, 2 * tq, tk), F32), pltpu.VMEM((2 * tq, 1), F32),
            pltpu.VMEM((2 * tq, 2 * LANES), F32)]


def _diff_attn_kernel(lam_ref, g_ref, q_ref, k_ref, v_ref, bd_ref, bp_ref, o_ref,
                      s_ref, m_sc, acc_sc, *, tq, tk, lam_init):
    i = pl.program_id(2)
    q = q_ref[0]
    lane = lax.broadcasted_iota(jnp.int32, (1, LANES), 1)
    zero = jnp.zeros_like(q)
    qq = jnp.concatenate([jnp.where(lane < HEAD_DIM, q, zero),
                          jnp.where(lane >= HEAD_DIM, q, zero)], axis=0)
    m_sc[...] = jnp.full(m_sc.shape, MASK_VALUE, F32)
    acc_sc[...] = jnp.zeros(acc_sc.shape, F32)

    tile = lambda ref, j: ref[0, pl.ds(pl.multiple_of(j * tk, tk), tk), :]
    _run_causal_tiles(i, tq, qq, functools.partial(tile, k_ref), functools.partial(tile, v_ref),
                      s_ref, m_sc, acc_sc, lambda hh: bd_ref[0], lambda hh: bp_ref[0])

    acc = acc_sc[...]
    o = acc[:, :LANES] / acc[:, LANES:LANES + 1]
    lam = (jnp.exp(jnp.sum(lam_ref[0:1, :] * lam_ref[1:2, :], axis=-1, keepdims=True))
           - jnp.exp(jnp.sum(lam_ref[2:3, :] * lam_ref[3:4, :], axis=-1, keepdims=True)) + lam_init)
    od = o[:tq] - lam * o[tq:]
    ms = jnp.mean(jnp.square(od), -1, keepdims=True)
    o_ref[0] = ((od * lax.rsqrt(ms + NORM_EPS) * g_ref[...]) * (1.0 - lam_init)).astype(BF16)


def _diff_attention(qkv, lams, norm_g, bias_diag, bias_prev, lam_init):
    b, s, _ = qkv.shape
    tq, tk = _attn_tiles(s)
    t = BIAS_TILE
    nh = DIFF_HEADS
    whole = lambda col: pl.BlockSpec((1, s, LANES), lambda bi, h, i: (bi, 0, col + h))
    return pl.pallas_call(
        functools.partial(_diff_attn_kernel, tq=tq, tk=tk, lam_init=lam_init),
        grid=(b, nh, s // tq),
        in_specs=[
            pl.BlockSpec(lams.shape, lambda bi, h, i: (0, 0)),
            pl.BlockSpec(norm_g.shape, lambda bi, h, i: (0, 0)),
            pl.BlockSpec((1, tq, LANES), lambda bi, h, i: (bi, i, h)),
            whole(nh), whole(2 * nh),
            pl.BlockSpec((1, t, t), lambda bi, h, i: (h, 0, 0)),
            pl.BlockSpec((1, t, t), lambda bi, h, i: (h, 0, 0)),
        ],
        out_specs=pl.BlockSpec((1, tq, LANES), lambda bi, h, i: (bi, i, h)),
        out_shape=jax.ShapeDtypeStruct((b, s, BRANCH_WIDTH), BF16),
        scratch_shapes=_attn_scratch(tq, tk),
        compiler_params=_cparams(("parallel", "parallel", "arbitrary"), ATTN_VMEM_LIMIT),
        name="diff_attention",
    )(lams, norm_g, qkv, qkv, qkv, bias_diag, bias_prev)


def _moba_kernel(q_ref, km_ref, k_ref, v_ref, bd_ref, bp_ref, o_ref,
                 qa_sc, s_ref, m_sc, acc_sc, *, tq, tk):
    i = pl.program_id(2)
    shift = int(math.log2(MOBA_BLOCK))
    q = q_ref[0]
    lane = lax.broadcasted_iota(jnp.int32, (1, LANES), 1)
    row = lax.broadcasted_iota(jnp.int32, (tk, 1), 0)
    n_slot = -(-(k_ref.shape[1] // MOBA_BLOCK) // 8) * 8
    blk = lax.broadcasted_iota(jnp.int32, (n_slot, tq), 0).astype(F32)
    own = lax.shift_right_logical(i * tq + lax.broadcasted_iota(jnp.int32, (1, tq), 1),
                                  shift).astype(F32)
    zero = jnp.zeros_like(q)
    km = km_ref[0, :n_slot, :].astype(BF16)
    ninf = jnp.float32(-jnp.inf)
    for hh in range(2):
        qh = jnp.where((lane >= hh * HEAD_DIM) & (lane < (hh + 1) * HEAD_DIM), q, zero)
        gate = lax.dot_general(km, qh, _NT, preferred_element_type=F32)
        g = jnp.where(blk < own, gate, ninf)
        sel = blk >= own
        for _ in range(MOBA_TOPK):
            mx = jnp.max(g, axis=0, keepdims=True)
            first = jnp.min(jnp.where(g == mx, blk, float(LANES)), axis=0, keepdims=True)
            hit = blk == first
            sel = sel | (hit & (mx > ninf))
            g = jnp.where(hit, ninf, g)
        bias = jnp.concatenate([jnp.where(sel, 0.0, MASK_VALUE), jnp.zeros((LANES - n_slot, tq), F32)], axis=0)
        qa_sc[hh * tq:(hh + 1) * tq, :] = jnp.concatenate([qh, bias.T.astype(BF16)], axis=1)
    m_sc[...] = jnp.full(m_sc.shape, MASK_VALUE, F32)
    acc_sc[...] = jnp.zeros(acc_sc.shape, F32)

    def keys_of(j):
        start = pl.multiple_of(j * tk, tk)
        kblk = lax.shift_right_logical(start + row, shift)
        return jnp.concatenate([k_ref[0, pl.ds(start, tk), :],
                                jnp.where(lane == kblk, 1.0, 0.0).astype(BF16)], axis=1)

    values_of = lambda j: v_ref[0, pl.ds(pl.multiple_of(j * tk, tk), tk), :]
    _run_causal_tiles(i, tq, qa_sc, keys_of, values_of, s_ref, m_sc, acc_sc,
                      lambda hh: bd_ref[0, hh], lambda hh: bp_ref[0, hh])

    acc = acc_sc[...]
    o = acc[:, :LANES] / acc[:, LANES:LANES + 1]
    o_ref[0] = jnp.where(lane < HEAD_DIM, o[:tq], o[tq:]).astype(BF16)


def _moba_attention(qkv, kmean, bias_diag, bias_prev):
    b, s, _ = qkv.shape
    tq, tk = _attn_tiles(s)
    t = BIAS_TILE
    npair = MOBA_HEADS // 2
    whole = lambda col: pl.BlockSpec((1, s, LANES), lambda bi, p, i: (bi, 0, col + p))
    return pl.pallas_call(
        functools.partial(_moba_kernel, tq=tq, tk=tk),
        grid=(b, npair, s // tq),
        in_specs=[
            pl.BlockSpec((1, tq, LANES), lambda bi, p, i: (bi, i, p)),
            pl.BlockSpec((1, LANES, LANES), lambda bi, p, i: (bi, 0, p)),
            whole(npair), whole(2 * npair),
            pl.BlockSpec((1, 2, t, t), lambda bi, p, i: (p, 0, 0, 0)),
            pl.BlockSpec((1, 2, t, t), lambda bi, p, i: (p, 0, 0, 0)),
        ],
        out_specs=pl.BlockSpec((1, tq, LANES), lambda bi, p, i: (bi, i, p)),
        out_shape=jax.ShapeDtypeStruct((b, s, BRANCH_WIDTH), BF16),
        scratch_shapes=[pltpu.VMEM((2 * tq, 2 * LANES), BF16)] + _attn_scratch(tq, tk),
        compiler_params=_cparams(("parallel", "parallel", "arbitrary"), ATTN_VMEM_LIMIT),
        name="moba_attention",
    )(qkv, kmean, qkv, qkv, bias_diag, bias_prev)


def _merge_kernel(h_ref, oa_ref, ob_ref, oc_ref, od_ref, wg_ref, bg_ref, wb_ref, out_ref):
    h = h_ref[...]
    acc = None
    for br, o_ref in enumerate((oa_ref, ob_ref, oc_ref, od_ref)):
        gate = _sigmoid(jnp.dot(h, wg_ref[br], preferred_element_type=F32) + bg_ref[br])
        term = gate * jnp.dot(o_ref[...], wb_ref[br], preferred_element_type=F32)
        acc = term if acc is None else acc + term
    out_ref[...] = acc.astype(BF16)


def _merge(h, branches, w_gate, b_gate, w_branch):
    t, d = h.shape
    w = BRANCH_WIDTH
    tm = min(512, t)
    tn = 512
    nbr = len(branches)
    row = lambda j, i: (i, 0)
    return pl.pallas_call(
        _merge_kernel,
        grid=(d // tn, t // tm),
        in_specs=[pl.BlockSpec((tm, d), row)] + [pl.BlockSpec((tm, w), row)] * nbr + [
            pl.BlockSpec((nbr, d, tn), lambda j, i: (0, 0, j)),
            pl.BlockSpec((nbr, 1, tn), lambda j, i: (0, 0, j)),
            pl.BlockSpec((nbr, w, tn), lambda j, i: (0, 0, j)),
        ],
        out_specs=pl.BlockSpec((tm, tn), lambda j, i: (i, j)),
        out_shape=jax.ShapeDtypeStruct((t, d), BF16),
        compiler_params=_cparams(("parallel", "parallel")),
        name="gated_merge",
    )(h, *branches, w_gate, b_gate, w_branch)


def _route(aff, sel):
    ng, pg = N_EXPERT_GROUPS, EXPERTS_PER_GROUP
    srow = [sel[e:e + 1] for e in range(N_EXPERTS)]
    arow = [aff[e:e + 1] for e in range(N_EXPERTS)]
    best_score, best_grp = None, None
    for g in range(ng):
        rows = srow[g * pg:(g + 1) * pg]
        score = None
        for a in range(pg):
            for b in range(a + 1, pg):
                pair = rows[a] + rows[b]
                score = pair if score is None else jnp.maximum(score, pair)
        if best_score is None:
            best_score, best_grp = score, jnp.zeros_like(score, dtype=jnp.int32)
        else:
            better = score > best_score
            best_score = jnp.where(better, score, best_score)
            best_grp = jnp.where(better, g, best_grp)

    def pick(rows_by_group):
        out = rows_by_group[0]
        for g in range(1, ng):
            out = jnp.where(best_grp == g, rows_by_group[g], out)
        return out

    cand_s = [pick([srow[g * pg + k] for g in range(ng)]) for k in range(pg)]
    cand_a = [pick([arow[g * pg + k] for g in range(ng)]) for k in range(pg)]

    def argmax_first(vals, exclude):
        bv, bi, ba = None, None, None
        for k in range(pg):
            v = vals[k] if exclude is None else jnp.where(exclude == k, -jnp.inf, vals[k])
            if bv is None:
                bv, bi, ba = v, jnp.zeros_like(best_grp), cand_a[k]
            else:
                better = v > bv
                bv = jnp.where(better, v, bv)
                bi = jnp.where(better, k, bi)
                ba = jnp.where(better, cand_a[k], ba)
        return bi, ba

    i0, a0 = argmax_first(cand_s, None)
    i1, a1 = argmax_first(cand_s, i0)
    tot = a0 + a1
    return best_grp * pg + i0, best_grp * pg + i1, a0 / tot, a1 / tot


def _outproj_kernel(m_ref, x_ref, wo_ref, g1_ref, lg_ref, lb_ref, sc2_ref, sh2_ref, wr_ref, br_ref,
                    x1_ref, h2_ref, ri_ref, rw_ref, cnt_ref, *, tm):
    @pl.when((pl.program_id(0) == 0) & (pl.program_id(1) == 0))
    def _():
        cnt_ref[...] = jnp.zeros(cnt_ref.shape, F32)

    rc = min(OUTPROJ_ROW_CHUNK, tm)
    erow = lax.broadcasted_iota(jnp.int32, (N_EXPERTS, rc), 0)
    earlier = (lax.broadcasted_iota(jnp.int32, (rc, rc), 0)
               < lax.broadcasted_iota(jnp.int32, (rc, rc), 1)).astype(BF16)
    for r0 in range(0, tm, rc):
        rows = slice(r0, r0 + rc)
        y = jnp.dot(m_ref[0, rows], wo_ref[...], preferred_element_type=F32)
        x1 = _layer_norm(ALPHA * x_ref[0, rows] + (1.0 + g1_ref[0]) * y, lg_ref[...], lb_ref[...])
        x1_ref[0, rows] = x1
        h2 = x1 * (1.0 + sc2_ref[0]) + sh2_ref[0]
        h2_ref[0, rows] = h2
        logits = lax.dot_general(wr_ref[...], h2.astype(BF16), _NT, preferred_element_type=F32)
        aff = _sigmoid(logits)
        e0, e1, w0, w1 = _route(aff, aff + br_ref[...])

        hit0, hit1 = erow == e0, erow == e1
        hits = jnp.where(hit0 | hit1, 1.0, 0.0)
        before = jnp.dot(hits.astype(BF16), earlier, preferred_element_type=F32) + cnt_ref[:, 0:1]
        rank = lambda hit: jnp.sum(jnp.where(hit, before, 0.0), axis=0, keepdims=True).astype(jnp.int32)
        cnt_ref[...] = cnt_ref[...] + jnp.sum(hits, axis=1, keepdims=True)
        ri_ref[0, :, rows] = jnp.concatenate(
            [e0, e1, rank(hit0), rank(hit1), jnp.zeros((4, rc), jnp.int32)], axis=0)
        rw_ref[0, :, rows] = jnp.concatenate([w0, w1, jnp.zeros((6, rc), F32)], axis=0)


def _outproj(merged, x, w_out, g1, ln_g, ln_b, sc2, sh2, w_router_t, b_router):
    b, s, d = x.shape
    tm = min(512, s)
    row = lambda bi, i: (bi, i, 0)
    vec = lambda bi, i: (bi, 0, 0)
    full = lambda a: pl.BlockSpec(a.shape, lambda bi, i: (0,) * a.ndim)
    lg, lb = ln_g.reshape(1, d), ln_b.reshape(1, d)
    br = b_router.reshape(N_EXPERTS, 1)
    return pl.pallas_call(
        functools.partial(_outproj_kernel, tm=tm),
        grid=(b, s // tm),
        in_specs=[
            pl.BlockSpec((1, tm, d), row), pl.BlockSpec((1, tm, d), row), full(w_out),
            pl.BlockSpec((1, 1, d), vec), full(lg), full(lb),
            pl.BlockSpec((1, 1, d), vec), pl.BlockSpec((1, 1, d), vec),
            full(w_router_t), full(br),
        ],
        out_specs=[
            pl.BlockSpec((1, tm, d), row), pl.BlockSpec((1, tm, d), row),
            pl.BlockSpec((1, 8, tm), lambda bi, i: (bi, 0, i)),
            pl.BlockSpec((1, 8, tm), lambda bi, i: (bi, 0, i)),
            pl.BlockSpec((N_EXPERTS, LANES), lambda bi, i: (0, 0)),
        ],
        out_shape=[
            jax.ShapeDtypeStruct((b, s, d), F32), jax.ShapeDtypeStruct((b, s, d), F32),
            jax.ShapeDtypeStruct((b, 8, s), jnp.int32), jax.ShapeDtypeStruct((b, 8, s), F32),
            jax.ShapeDtypeStruct((N_EXPERTS, LANES), F32),
        ],
        compiler_params=_cparams(("arbitrary", "arbitrary")),
        name="outproj_ln_router",
    )(merged.reshape(b, s, d), x, w_out, g1, lg, lb, sc2, sh2, w_router_t, br)


def _row_copy(src, src_row, dst, dst_row, sem):
    return pltpu.make_async_copy(src.at[pl.ds(src_row, 1)], dst.at[pl.ds(dst_row, 1)], sem)


def _dispatch_kernel(tail_ref, nu_ref, dest_ref, h2_ref, xs_ref, zeros_sc, sem, zsem, *, rows):
    @pl.when(pl.program_id(0) == 0)
    def _():
        zeros_sc[...] = jnp.zeros(zeros_sc.shape, F32)
        fill = lambda row0: pltpu.make_async_copy(
            zeros_sc, xs_ref.at[pl.ds(pl.multiple_of(row0, SLOT_BLOCK), SLOT_BLOCK)], zsem)
        unused = (nu_ref[0], xs_ref.shape[0] // SLOT_BLOCK)
        for e in range(N_EXPERTS):
            pl.when(tail_ref[e] >= 0)(lambda e=e: fill(tail_ref[e]).start())
        lax.fori_loop(*unused, lambda b, c: (fill(b * SLOT_BLOCK).start(), c)[1], 0)
        for e in range(N_EXPERTS):
            pl.when(tail_ref[e] >= 0)(lambda e=e: fill(tail_ref[e]).wait())
        lax.fori_loop(*unused, lambda b, c: (fill(b * SLOT_BLOCK).wait(), c)[1], 0)

    def issue(r, carry):
        for k in range(TOP_K):
            _row_copy(h2_ref, r, xs_ref, dest_ref[0, 0, TOP_K * r + k], sem).start()
        return carry

    lax.fori_loop(0, rows, issue, 0, unroll=4)
    for k in range(TOP_K):
        pltpu.make_async_copy(h2_ref, xs_ref.at[pl.ds(0, rows)], sem).wait()


def _dispatch(dest, tail_start, n_used, h2, n_slots):
    t, d = h2.shape
    rows = min(512, t)
    return pl.pallas_call(
        functools.partial(_dispatch_kernel, rows=rows),
        grid_spec=pltpu.PrefetchScalarGridSpec(
            num_scalar_prefetch=2,
            grid=(t // rows,),
            in_specs=[
                pl.BlockSpec((1, 1, TOP_K * rows), lambda i, tail, nu: (i, 0, 0), memory_space=pltpu.SMEM),
                pl.BlockSpec((rows, d), lambda i, tail, nu: (i, 0)),
            ],
            out_specs=pl.BlockSpec(memory_space=pl.ANY),
            scratch_shapes=[pltpu.VMEM((SLOT_BLOCK, d), F32), pltpu.SemaphoreType.DMA(()),
                            pltpu.SemaphoreType.DMA(())],
        ),
        out_shape=jax.ShapeDtypeStruct((n_slots, d), F32),
        compiler_params=_cparams(("arbitrary",)),
        name="moe_dispatch",
    )(tail_start, n_used, dest.reshape(t // rows, 1, TOP_K * rows), h2)


def _cast_kernel(w_ref, o_ref):
    o_ref[0] = w_ref[0, 0].astype(BF16)


def _cast_layer(w, layer, rows):
    _, e, r, c = w.shape
    return pl.pallas_call(
        _cast_kernel,
        grid=(e, r // rows),
        in_specs=[pl.BlockSpec((1, 1, rows, c), lambda ei, i: (layer, ei, i, 0))],
        out_specs=pl.BlockSpec((1, rows, c), lambda ei, i: (ei, i, 0)),
        out_shape=jax.ShapeDtypeStruct((e, r, c), BF16),
        compiler_params=_cparams(("parallel", "parallel")),
        name="expert_weight_cast",
    )(w)


def _expert_kernel(be_ref, nu_ref, xs_ref, wg_ref, wu_ref, wd_ref, ys_ref):
    del be_ref
    used = pl.program_id(0) < nu_ref[0]

    @pl.when(used)
    def _():
        xb = xs_ref[...].astype(BF16)
        g = jnp.dot(xb, wg_ref[0], preferred_element_type=F32)
        u = jnp.dot(xb, wu_ref[0], preferred_element_type=F32)
        act = ((g * _sigmoid(g)) * u).astype(BF16)
        ys_ref[...] = jnp.dot(act, wd_ref[0], preferred_element_type=F32)

    @pl.when(jnp.logical_not(used))
    def _():
        ys_ref[...] = jnp.zeros(ys_ref.shape, F32)


def _expert_ffn(blk_expert, n_used, xs, w_g, w_u, w_d):
    n_slots, d = xs.shape
    f = w_d.shape[1]
    blk = SLOT_BLOCK
    last = lambda b, nu: jnp.minimum(b, nu[0] - 1)
    wspec = lambda shape: pl.BlockSpec(shape, lambda b, be, nu: (be[last(b, nu)], 0, 0),
                                       pipeline_mode=pl.Buffered(1))
    return pl.pallas_call(
        _expert_kernel,
        grid_spec=pltpu.PrefetchScalarGridSpec(
            num_scalar_prefetch=2,
            grid=(n_slots // blk,),
            in_specs=[
                pl.BlockSpec((blk, d), lambda b, be, nu: (last(b, nu), 0)),
                wspec((1, d, f)), wspec((1, d, f)), wspec((1, f, d)),
            ],
            out_specs=pl.BlockSpec((blk, d), lambda b, be, nu: (b, 0)),
        ),
        out_shape=jax.ShapeDtypeStruct((n_slots, d), F32),
        compiler_params=_cparams(("arbitrary",)),
        name="moe_experts",
    )(blk_expert, n_used, xs, w_g, w_u, w_d)


def _combine_kernel(dest_ref, dest_next_ref, w_ref, x1_ref, g2_ref, lg_ref, lb_ref, ys_ref, o_ref,
                    buf, sem, *, rows):
    step = pl.program_id(0) * pl.num_programs(1) + pl.program_id(1)
    n_steps = pl.num_programs(0) * pl.num_programs(1)
    slot = step % 2

    def gather(dref, sl):
        def issue(r, carry):
            for k in range(TOP_K):
                _row_copy(ys_ref, dref[0, 0, TOP_K * r + k], buf.at[sl, k], r, sem.at[sl]).start()
            return carry
        lax.fori_loop(0, rows, issue, 0, unroll=4)

    pl.when(step == 0)(lambda: gather(dest_ref, 0))
    pl.when(step + 1 < n_steps)(lambda: gather(dest_next_ref, 1 - slot))
    for k in range(TOP_K):
        pltpu.make_async_copy(ys_ref.at[pl.ds(0, rows)], buf.at[slot, k], sem.at[slot]).wait()
    y = buf[slot, 0] * w_ref[:, 0:1] + buf[slot, 1] * w_ref[:, 1:2]
    o_ref[0] = _layer_norm(ALPHA * x1_ref[0] + (1.0 + g2_ref[0]) * y, lg_ref[...], lb_ref[...])


def _combine(dest, wts, x1, g2, ln_g, ln_b, ys):
    b, s, d = x1.shape
    rows = min(256, s)
    nt = s // rows
    lg, lb = ln_g.reshape(1, d), ln_b.reshape(1, d)
    full = lambda a: pl.BlockSpec(a.shape, lambda bi, i: (0,) * a.ndim)
    dest3 = dest.reshape(b * nt, 1, TOP_K * rows)
    smem = lambda ahead: pl.BlockSpec(
        (1, 1, TOP_K * rows), lambda bi, i: (jnp.minimum(bi * nt + i + ahead, b * nt - 1), 0, 0),
        memory_space=pltpu.SMEM)
    return pl.pallas_call(
        functools.partial(_combine_kernel, rows=rows),
        grid=(b, nt),
        in_specs=[
            smem(0), smem(1),
            pl.BlockSpec((rows, TOP_K), lambda bi, i: (bi * nt + i, 0)),
            pl.BlockSpec((1, rows, d), lambda bi, i: (bi, i, 0)),
            pl.BlockSpec((1, 1, d), lambda bi, i: (bi, 0, 0)),
            full(lg), full(lb),
            pl.BlockSpec(memory_space=pl.ANY),
        ],
        out_specs=pl.BlockSpec((1, rows, d), lambda bi, i: (bi, i, 0)),
        out_shape=jax.ShapeDtypeStruct((b, s, d), F32),
        scratch_shapes=[pltpu.VMEM((2, TOP_K, rows, d), F32), pltpu.SemaphoreType.DMA((2,))],
        compiler_params=_cparams(("arbitrary", "arbitrary")),
        name="moe_combine_ln",
    )(dest3, dest3, wts, x1, g2, lg, lb, ys)


def _slot_plan(e_flat, rank, counts, n_slots):
    blk = SLOT_BLOCK
    padded = (counts + blk - 1) // blk * blk
    pend = jnp.cumsum(padded)
    experts = jnp.arange(N_EXPERTS, dtype=jnp.int32)
    dest = jnp.sum(jnp.where(e_flat[:, None] == experts[None, :], (pend - padded)[None, :], 0), axis=1) + rank
    nblk = n_slots // blk
    blk_start = jnp.arange(nblk, dtype=jnp.int32) * blk
    blk_expert = jnp.minimum(jnp.sum(pend[None, :] <= blk_start[:, None], axis=1), N_EXPERTS - 1)
    tail_start = jnp.where(padded > 0, pend - blk, -1)
    return (dest.astype(jnp.int32), blk_expert.astype(jnp.int32), (pend[-1:] // blk).astype(jnp.int32),
            tail_start.astype(jnp.int32))


def _t5_bucket(dist):
    max_exact = REL_BUCKETS // 2
    d = jnp.maximum(dist, 1).astype(F32)
    large = max_exact + (jnp.log(d / max_exact) / math.log(REL_MAX_DIST / max_exact)
                         * (REL_BUCKETS - max_exact)).astype(jnp.int32)
    large = jnp.minimum(large, REL_BUCKETS - 1)
    return jnp.where(dist < max_exact, dist, large)


def _bias_tiles(rel_table, t):
    r = jnp.arange(t, dtype=jnp.int32)[:, None]
    c = jnp.arange(t, dtype=jnp.int32)[None, :]
    dd = r - c

    def lookup(dist):
        onehot = (_t5_bucket(dist)[:, :, None] == jnp.arange(REL_BUCKETS, dtype=jnp.int32)).astype(F32)
        return jnp.einsum('rck,kh->hrc', onehot, rel_table, precision=lax.Precision.HIGHEST)

    diag = jnp.where(dd >= 0, lookup(jnp.maximum(dd, 0)), MASK_VALUE)
    return diag, lookup(dd + t)


def kernel(x, c, rel_bias, w_router, b_router, w_ada, b_ada, w_in, lam_q1, lam_k1, lam_q2, lam_k2,
           diff_norm_g, w_pool, pool_scale, sgu_ln_g, sgu_ln_b, w_spatial, b_spatial, w_branch,
           w_gate, b_gate, w_out, ln1_g, ln1_b, w_e_gate, w_e_up, w_e_down, ln2_g, ln2_b):
    b, s, d = x.shape
    t = b * s
    assert d == D_MODEL and s % MOBA_BLOCK == 0 and s // MOBA_BLOCK <= LANES
    assert int(16 * 8 ** (15 / 16)) + 1 < MOBA_BLOCK and REL_BUCKETS == 32 and REL_MAX_DIST == 128

    assert s % min(ATTN_Q_TILE, s) == 0 and BIAS_TILE == MOBA_BLOCK
    diag, prev = _bias_tiles(rel_bias - rel_bias[REL_BUCKETS - 1][None, :], BIAS_TILE)
    a_diag, a_prev = diag[:DIFF_HEADS], prev[:DIFF_HEADS]
    pair = lambda a: a[DIFF_HEADS:].reshape(MOBA_HEADS // 2, 2, BIAS_TILE, BIAS_TILE)
    d_diag, d_prev = pair(diag), pair(prev)

    mod = _ada_mod(c, w_ada, b_ada)
    w_router_t = w_router.T.astype(BF16)
    n_slots = (-(-t * TOP_K // SLOT_BLOCK) + N_EXPERTS) * SLOT_BLOCK

    for layer in range(DEPTH):
        sh1, sc1, g1, sh2, sc2, g2 = (m[:, None, :] for m in jnp.split(mod[layer], 6, axis=-1))
        lam_init = 0.8 - 0.6 * math.exp(-0.3 * layer)
        lams = jnp.stack([lam_q1[layer], lam_k1[layer], lam_q2[layer], lam_k2[layer]])

        h, qkv_a, zpuv, qkv_d, kmean = _inproj(x, sc1, sh1, w_in[layer].astype(BF16))
        o_b, o_c = _branches(zpuv, w_pool[layer], pool_scale[layer], sgu_ln_g[layer], sgu_ln_b[layer],
                             w_spatial[layer], b_spatial[layer])
        o_a = _diff_attention(qkv_a, lams, diff_norm_g[layer].reshape(1, 2 * HEAD_DIM),
                              a_diag, a_prev, lam_init)
        kmean = kmean.reshape(b, s // MOBA_BLOCK, BRANCH_WIDTH)
        kmean = jnp.pad(kmean, ((0, 0), (0, LANES - s // MOBA_BLOCK), (0, 0)))
        o_d = _moba_attention(qkv_d, kmean, d_diag, d_prev)

        flat = lambda a: a.reshape(t, a.shape[-1])
        merged = _merge(flat(h), [flat(o_a), flat(o_b), flat(o_c), flat(o_d)],
                        w_gate[layer].astype(BF16), b_gate[layer][:, None, :], w_branch[layer].astype(BF16))
        x1, h2, ridx, rwts, counts = _outproj(
            merged, x, w_out[layer].astype(BF16), g1, ln1_g[layer], ln1_b[layer], sc2, sh2, w_router_t, b_router)

        per_assignment = lambda rows: jnp.transpose(rows, (0, 2, 1)).reshape(t * TOP_K)
        wts = per_assignment(rwts[:, :TOP_K, :]).reshape(t, TOP_K)
        dest, blk_expert, n_used, tail_start = _slot_plan(
            per_assignment(ridx[:, :TOP_K, :]), per_assignment(ridx[:, TOP_K:2 * TOP_K, :]),
            counts[:, 0].astype(jnp.int32), n_slots)
        xs = _dispatch(dest, tail_start, n_used, h2.reshape(t, d), n_slots)
        ys = _expert_ffn(blk_expert, n_used, xs, _cast_layer(w_e_gate, layer, 512),
                         _cast_layer(w_e_up, layer, 512), _cast_layer(w_e_down, layer, EXPERT_FF // 4))
        x = _combine(dest, wts, x1, g2, ln2_g[layer], ln2_b[layer], ys)
    return x
```
